```python
import jax, jax.numpy as jnp
from jax import lax
import numpy as np

D_MODEL = 1024
BATCH = 1
SEQ = 16384
DEPTH = 1
DEC_BATCH = 128
DEC_SEQ = 1
PAST_LEN = 8192
PAGE_SIZE = 128

A_HEADS = 8
HEAD_DIM = 64
A_WIDTH = A_HEADS * HEAD_DIM
IDX_HEADS = 4
IDX_DIM = 64
TOPK_MAX = 256
ROT_DIM = HEAD_DIM // 4
ROPE_THETA = 500000.0
Q_BLOCK = 128
B_HEADS = 8
B_HEAD_DIM = 64
B_WIDTH = B_HEADS * B_HEAD_DIM
DECAY_LORA = 64
AAA_LORA = 64
GATE_LORA = 128
GN_EPS = B_HEAD_DIM * 1e-5
SHIFT_W = 3 * B_WIDTH + DECAY_LORA + AAA_LORA + GATE_LORA
N_GROUPS = 4
EXPERTS_PER_GROUP = 8
N_EXPERTS = N_GROUPS * EXPERTS_PER_GROUP
TOP_K_IN_GROUP = 2
D_EXPERT = 512
RMS_EPS = 1e-6
IN_SPLITS = (A_WIDTH, A_WIDTH, A_WIDTH, IDX_HEADS * IDX_DIM, IDX_DIM, IDX_HEADS, SHIFT_W, 2 * D_MODEL)
D_IN = sum(IN_SPLITS)
SHIFT_SPLITS = (B_WIDTH, B_WIDTH, B_WIDTH, DECAY_LORA, AAA_LORA, GATE_LORA)

kernel_name = 'hybrid_dsa_rwkv7_hmoe_step'


def split_cols(z, sizes):
    return jnp.split(z, np.cumsum(sizes)[:-1].tolist(), axis=-1)


def rms_norm(x, g):
    xf = x.astype(jnp.float32)
    y = xf * lax.rsqrt(jnp.mean(xf * xf, -1, keepdims=True) + RMS_EPS)
    return (y * g.astype(jnp.float32)).astype(x.dtype)


def partial_rope(x, pos):
    half = ROT_DIM // 2
    inv = ROPE_THETA ** (-jnp.arange(half, dtype=jnp.float32) / half)
    ang = pos.astype(jnp.float32)[:, None] * inv[None, :]
    cos = jnp.cos(ang)[:, None, :]
    sin = jnp.sin(ang)[:, None, :]
    xr = x[..., :ROT_DIM].astype(jnp.float32)
    x1, x2 = xr[..., :half], xr[..., half:]
    rot = jnp.concatenate([x1 * cos - x2 * sin, x2 * cos + x1 * sin], -1).astype(x.dtype)
    return jnp.concatenate([rot, x[..., ROT_DIM:]], -1)


def indexer_select(q_idx, w_idx, k_idx, q_pos, k_sel):
    s = jax.nn.relu(jnp.einsum('bqhd,bld->bqhl', q_idx, k_idx).astype(jnp.float32))
    score = jnp.einsum('bqhl,bqh->bql', s, w_idx.astype(jnp.float32))
    key_pos = jnp.arange(k_idx.shape[1])
    causal = key_pos[None, :] <= q_pos[:, None]
    score = jnp.where(causal[None], score, -jnp.inf)
    _, sel = lax.top_k(score, k_sel)
    valid = sel <= q_pos[None, :, None]
    return sel, valid


def sparse_attend(q, k_sel, v_sel, valid):
    logits = jnp.einsum('bqhd,bqkhd->bqhk', q, k_sel).astype(jnp.float32) * (HEAD_DIM ** -0.5)
    logits = jnp.where(valid[:, :, None, :], logits, -jnp.inf)
    p = jax.nn.softmax(logits, -1).astype(v_sel.dtype)
    return jnp.einsum('bqhk,bqkhd->bqhd', p, v_sel)


def dsa_prompt(q, k, v, q_idx, w_idx, k_idx):
    Bsz, T = q.shape[:2]
    k_sel_n = min(TOPK_MAX, T // 4)
    nb = T // Q_BLOCK
    gather = jax.vmap(lambda kv, ix: kv[ix])

    def block(args):
        qb, qib, wib, b = args
        q_pos = b * Q_BLOCK + jnp.arange(Q_BLOCK)
        sel, valid = indexer_select(qib, wib, k_idx, q_pos, k_sel_n)
        return sparse_attend(qb, gather(k, sel), gather(v, sel), valid)

    to_blocks = lambda z: jnp.moveaxis(z.reshape(Bsz, nb, Q_BLOCK, *z.shape[2:]), 1, 0)
    out = lax.map(block, (to_blocks(q), to_blocks(q_idx), to_blocks(w_idx), jnp.arange(nb)))
    return jnp.moveaxis(out, 0, 1).reshape(Bsz, T, A_HEADS, HEAD_DIM)


def dsa_sample(q, k_new, v_new, q_idx, w_idx, k_idx_new, layer, cache_k, cache_v, cache_idx_k, page_table):
    Bd, S = q.shape[:2]
    past = page_table.shape[1] * PAGE_SIZE
    k_sel_n = min(TOPK_MAX, (past + S) // 4)
    k_idx_past = cache_idx_k[layer, page_table].reshape(Bd, past, IDX_DIM)
    k_idx_all = jnp.concatenate([k_idx_past, k_idx_new], 1)
    q_pos = past + jnp.arange(S)
    sel, valid = indexer_select(q_idx, w_idx, k_idx_all, q_pos, k_sel_n)
    in_past = (sel < past)[..., None, None]
    sel_past = jnp.minimum(sel, past - 1)
    phys = jax.vmap(lambda pt, ix: pt[ix])(page_table, sel_past // PAGE_SIZE)
    off = sel_past % PAGE_SIZE
    sel_new = jnp.clip(sel - past, 0, S - 1)
    gather_new = jax.vmap(lambda kv, ix: kv[ix])
    k_sel = jnp.where(in_past, cache_k[layer, phys, off], gather_new(k_new, sel_new))
    v_sel = jnp.where(in_past, cache_v[layer, phys, off], gather_new(v_new, sel_new))
    return sparse_attend(q, k_sel, v_sel, valid)


def token_shift(u, prev, mu):
    u_prev = jnp.concatenate([prev[:, None, :].astype(u.dtype), u[:, :-1]], 1)
    return u + (u_prev - u) * mu, u[:, -1]


def rwkv7_branch(shifted, wkv0, w0, w_lora_up, a0, a_lora_up, g_lora_up, k_k, k_a, r_k, gn_g, gn_b):
    f32 = jnp.float32
    Bsz, T, _ = shifted.shape
    r, k, v, xw, xa, xg = split_cols(shifted, SHIFT_SPLITS)
    logw = -jax.nn.softplus(-(w0 + jnp.tanh(xw) @ w_lora_up)) - 0.5
    decay = jnp.exp(-jnp.exp(logw.astype(f32)))
    a = jax.nn.sigmoid((a0 + xa @ a_lora_up).astype(f32))
    g = jax.nn.sigmoid(xg) @ g_lora_up
    hd = lambda z: z.astype(f32).reshape(Bsz, T, B_HEADS, B_HEAD_DIM)
    kk = hd(k * k_k)
    kk = kk / jnp.maximum(jnp.sqrt(jnp.sum(kk * kk, -1, keepdims=True)), 1e-12)
    r, v, a, decay = hd(r), hd(v), hd(a), hd(decay)
    k_a_h = k_a.astype(f32).reshape(B_HEADS, B_HEAD_DIM)
    k = hd(k) * (1.0 + (a - 1.0) * k_a_h)

    def step(S_, inp):
        r_t, w_t, k_t, v_t, kk_t, a_t = inp
        sa = jnp.einsum('bhij,bhj->bhi', S_, -kk_t)
        S_ = S_ * w_t[:, :, None, :] + sa[..., :, None] * (kk_t * a_t)[..., None, :] + v_t[..., :, None] * k_t[..., None, :]
        return S_, jnp.einsum('bhij,bhj->bhi', S_, r_t)

    seq = tuple(jnp.moveaxis(z, 1, 0) for z in (r, decay, k, v, kk, a))
    S_fin, y = lax.scan(step, wkv0.astype(f32), seq)
    y = jnp.moveaxis(y, 0, 1)
    mu = jnp.mean(y, -1, keepdims=True)
    var = jnp.mean(jnp.square(y - mu), -1, keepdims=True)
    yn = ((y - mu) * lax.rsqrt(var + GN_EPS)).reshape(Bsz, T, B_WIDTH) * gn_g.astype(f32) + gn_b.astype(f32)
    bonus = (jnp.sum(r * k * r_k.astype(f32), -1, keepdims=True) * v).reshape(Bsz, T, B_WIDTH)
    out = (yn + bonus) * g.astype(f32)
    return out.astype(shifted.dtype), S_fin.astype(wkv0.dtype)


def hybrid_mixer(x, pos, shift0, wkv0, attend_fn, norm_mix, w_in, mu_shift, w0, w_lora_up, a0, a_lora_up,
                 g_lora_up, k_k, k_a, r_k, gn_g, gn_b, w_branch_a, w_branch_b, w_out):
    Bsz, T = x.shape[:2]
    xn = rms_norm(x, norm_mix)
    q, k, v, qi, ki, wi, shift_in, gates = split_cols(xn @ w_in, IN_SPLITS)
    q = partial_rope(q.reshape(Bsz, T, A_HEADS, HEAD_DIM), pos)
    k = partial_rope(k.reshape(Bsz, T, A_HEADS, HEAD_DIM), pos)
    v = v.reshape(Bsz, T, A_HEADS, HEAD_DIM)
    qi = partial_rope(qi.reshape(Bsz, T, IDX_HEADS, IDX_DIM), pos)
    ki = partial_rope(ki[:, :, None, :], pos)[:, :, 0]
    wi = wi * (IDX_HEADS ** -0.5 * IDX_DIM ** -0.5)
    attn = attend_fn(q, k, v, qi, wi, ki).reshape(Bsz, T, A_WIDTH)
    shifted, shift_last = token_shift(shift_in, shift0, mu_shift)
    rw, wkv_fin = rwkv7_branch(shifted, wkv0, w0, w_lora_up, a0, a_lora_up, g_lora_up, k_k, k_a, r_k, gn_g, gn_b)
    gate_a, gate_b = split_cols(jax.nn.sigmoid(gates), (D_MODEL, D_MODEL))
    merged = gate_a * (attn @ w_branch_a) + gate_b * (rw @ w_branch_b)
    h = x + merged @ w_out
    return h, k, v, ki, wkv_fin, shift_last.astype(shift0.dtype)


def hier_moe(h, norm_ffn, w_router_group, b_router_group, w_router_expert, b_router_expert, w_gate, w_up, w_down):
    f32 = jnp.float32
    Bsz, T, D = h.shape
    hn = rms_norm(h, norm_ffn).reshape(Bsz * T, D)
    gprob = jax.nn.softmax((hn @ w_router_group).astype(f32) + b_router_group.astype(f32), -1)
    g_sel = jnp.argmax(gprob, -1)
    p_group = jnp.take_along_axis(gprob, g_sel[:, None], -1)
    elog = ((hn @ w_router_expert).astype(f32) + b_router_expert.astype(f32)).reshape(-1, N_GROUPS, EXPERTS_PER_GROUP)
    elog_g = jnp.take_along_axis(elog, g_sel[:, None, None], 1)[:, 0]
    top_v, top_i = lax.top_k(elog_g, TOP_K_IN_GROUP)
    wts = jax.nn.softmax(top_v, -1) * p_group
    eid = g_sel[:, None] * EXPERTS_PER_GROUP + top_i
    combine = jnp.sum(jax.nn.one_hot(eid, N_EXPERTS, dtype=f32) * wts[..., None], 1)

    def expert_step(acc, ew):
        wg, wu, wd, c = ew
        y = (jax.nn.silu(hn @ wg) * (hn @ wu)) @ wd
        return acc + c[:, None].astype(y.dtype) * y, None

    y, _ = lax.scan(expert_step, jnp.zeros_like(hn), (w_gate, w_up, w_down, combine.T))
    return h + y.reshape(Bsz, T, D)


def setup_inputs(seed: int = 0) -> dict:
    key = jax.random.key(seed)
    ks = iter(jax.random.split(key, 48))
    nrm = lambda shape, scale: jax.random.normal(next(ks), shape, jnp.float32) * scale
    n_pages = PAST_LEN // PAGE_SIZE
    n_used = DEC_BATCH * n_pages
    n_pool = n_used + max(1, n_used // 4)
    L, D = DEPTH, D_MODEL
    return {
        'x_prompt': nrm((BATCH, SEQ, D), 1.0),
        'x_sample': nrm((DEC_BATCH, DEC_SEQ, D), 1.0),
        'cache_k': nrm((L, n_pool, PAGE_SIZE, A_HEADS, HEAD_DIM), 1.0),
        'cache_v': nrm((L, n_pool, PAGE_SIZE, A_HEADS, HEAD_DIM), 1.0),
        'cache_idx_k': nrm((L, n_pool, PAGE_SIZE, IDX_DIM), 1.0),
        'state_wkv': nrm((L, DEC_BATCH, B_HEADS, B_HEAD_DIM, B_HEAD_DIM), 0.3),
        'state_shift': nrm((L, DEC_BATCH, SHIFT_W), 1.0),
        'page_table': jax.random.permutation(next(ks), n_pool)[:n_used].reshape(DEC_BATCH, n_pages).astype(jnp.int32),
        'norm_mix': 1.0 + nrm((L, D), 0.05),
        'w_in': nrm((L, D, D_IN), D ** -0.5),
        'mu_shift': jax.random.uniform(next(ks), (L, SHIFT_W), jnp.float32),
        'w0': jax.random.uniform(next(ks), (L, B_WIDTH), jnp.float32, minval=-6.0, maxval=-1.0),
        'w_lora_up': nrm((L, DECAY_LORA, B_WIDTH), 0.5 * DECAY_LORA ** -0.5),
        'a0': nrm((L, B_WIDTH), 0.3),
        'a_lora_up': nrm((L, AAA_LORA, B_WIDTH), AAA_LORA ** -0.5),
        'g_lora_up': nrm((L, GATE_LORA, B_WIDTH), GATE_LORA ** -0.5),
        'k_k': 0.85 + nrm((L, B_WIDTH), 0.05),
        'k_a': 1.0 + nrm((L, B_WIDTH), 0.05),
        'r_k': nrm((L, B_HEADS, B_HEAD_DIM), 0.1),
        'gn_g': 1.0 + nrm((L, B_WIDTH), 0.05),
        'gn_b': nrm((L, B_WIDTH), 0.02),
        'w_branch_a': nrm((L, A_WIDTH, D), A_WIDTH ** -0.5),
        'w_branch_b': nrm((L, B_WIDTH, D), B_WIDTH ** -0.5),
        'w_out': nrm((L, D, D), D ** -0.5),
        'norm_ffn': 1.0 + nrm((L, D), 0.05),
        'w_router_group': nrm((L, D, N_GROUPS), D ** -0.5),
        'b_router_group': nrm((L, N_GROUPS), 0.01),
        'w_router_expert': nrm((L, D, N_EXPERTS), D ** -0.5),
        'b_router_expert': nrm((L, N_EXPERTS), 0.01),
        'w_gate': nrm((L, N_EXPERTS, D, D_EXPERT), D ** -0.5),
        'w_up': nrm((L, N_EXPERTS, D, D_EXPERT), D ** -0.5),
        'w_down': nrm((L, N_EXPERTS, D_EXPERT, D), D_EXPERT ** -0.5),
        'norm_final': 1.0 + nrm((D,), 0.05),
    }


def reference(x_prompt, x_sample, cache_k, cache_v, cache_idx_k, state_wkv, state_shift, page_table,
              norm_mix, w_in, mu_shift, w0, w_lora_up, a0, a_lora_up, g_lora_up, k_k, k_a, r_k, gn_g, gn_b,
              w_branch_a, w_branch_b, w_out, norm_ffn, w_router_group, b_router_group, w_router_expert,
              b_router_expert, w_gate, w_up, w_down, norm_final):
    Bp, Tp = x_prompt.shape[:2]
    Ts = x_sample.shape[1]
    past = page_table.shape[1] * PAGE_SIZE
    pos_p = jnp.arange(Tp)
    pos_s = past + jnp.arange(Ts)
    hp, hs = x_prompt, x_sample
    outs_p, outs_s = [], []
    for l in range(DEPTH):
        mix_w = (norm_mix[l], w_in[l], mu_shift[l], w0[l], w_lora_up[l], a0[l], a_lora_up[l], g_lora_up[l],
                 k_k[l], k_a[l], r_k[l], gn_g[l], gn_b[l], w_branch_a[l], w_branch_b[l], w_out[l])
        ffn_w = (norm_ffn[l], w_router_group[l], b_router_group[l], w_router_expert[l], b_router_expert[l],
                 w_gate[l], w_up[l], w_down[l])
        shift0_p = jnp.zeros((Bp, SHIFT_W), x_prompt.dtype)
        wkv0_p = jnp.zeros((Bp, B_HEADS, B_HEAD_DIM, B_HEAD_DIM), x_prompt.dtype)
        hp, kp, vp, ip, wp, sp = hybrid_mixer(hp, pos_p, shift0_p, wkv0_p, dsa_prompt, *mix_w)
        hp = hier_moe(hp, *ffn_w)
        attend_s = lambda q, k, v, qi, wi, ki, l=l: dsa_sample(q, k, v, qi, wi, ki, l, cache_k, cache_v, cache_idx_k, page_table)
        hs, ks_, vs, is_, ws, ss = hybrid_mixer(hs, pos_s, state_shift[l], state_wkv[l], attend_s, *mix_w)
        hs = hier_moe(hs, *ffn_w)
        outs_p.append((kp, vp, ip, wp, sp))
        outs_s.append((ks_, vs, is_, ws, ss))
    y_prompt = rms_norm(hp, norm_final)
    y_sample = rms_norm(hs, norm_final)
    stk = lambda outs, i: jnp.stack([o[i] for o in outs], 0)
    return (y_prompt, y_sample,
            stk(outs_p, 0), stk(outs_p, 1), stk(outs_p, 2), stk(outs_p, 3), stk(outs_p, 4),
            stk(outs_s, 0), stk(outs_s, 1), stk(outs_s, 2), stk(outs_s, 3), stk(outs_s, 4))
```

```python
import functools

import jax
import jax.numpy as jnp
import numpy as np
from jax import lax
from jax.experimental import pallas as pl
from jax.experimental.pallas import tpu as pltpu

F32 = jnp.float32
BF16 = jnp.bfloat16

D_MODEL = 1024
PAGE_SIZE = 128
A_HEADS = 8
HEAD_DIM = 64
A_WIDTH = A_HEADS * HEAD_DIM
IDX_HEADS = 4
IDX_DIM = 64
TOPK_MAX = 256
ROT_DIM = HEAD_DIM // 4
ROPE_THETA = 500000.0
B_HEADS = 8
B_HEAD_DIM = 64
B_WIDTH = B_HEADS * B_HEAD_DIM
DECAY_LORA = 64
AAA_LORA = 64
GATE_LORA = 128
GN_EPS = B_HEAD_DIM * 1e-5
SHIFT_W = 3 * B_WIDTH + DECAY_LORA + AAA_LORA + GATE_LORA
N_GROUPS = 4
EXPERTS_PER_GROUP = 8
N_EXPERTS = N_GROUPS * EXPERTS_PER_GROUP
D_EXPERT = 512
RMS_EPS = 1e-6

LANES = 128
SUBLANES = 8
VMEM_LIMIT_BYTES = 56 * 1024 * 1024

QKV_END = 3 * A_WIDTH
QI_END = QKV_END + IDX_HEADS * IDX_DIM
KIWI_END = QI_END + LANES
SHIFT_END = KIWI_END + SHIFT_W
D_IN_PACKED = SHIFT_END + 2 * D_MODEL
IDX_W_SCALE = IDX_HEADS ** -0.5 * IDX_DIM ** -0.5
NEG_BIG = -1e30
CHUNK = 64


def _cparams(sem):
    return pltpu.CompilerParams(dimension_semantics=sem, vmem_limit_bytes=VMEM_LIMIT_BYTES)


def _dot(a, b):
    return jnp.dot(a, b, preferred_element_type=F32)


def _dot_t(a, b):
    return lax.dot_general(a, b, (((1,), (1,)), ((), ())), preferred_element_type=F32)


def _dotx(a, b):
    return jnp.dot(a, b, preferred_element_type=F32, precision=lax.Precision.HIGHEST)


def _dotx_t(a, b):
    return lax.dot_general(a, b, (((1,), (1,)), ((), ())), preferred_element_type=F32,
                           precision=lax.Precision.HIGHEST)


def _rms(x, g):
    return x * lax.rsqrt(jnp.mean(x * x, axis=-1, keepdims=True) + RMS_EPS) * g


def _rope_slab(x, c, sa, sb):
    return x * c + pltpu.roll(x, ROT_DIM // 2, 1) * sa + pltpu.roll(x, LANES - ROT_DIM // 2, 1) * sb


def _inproj_kernel(seq_mode, x_ref, g_ref, w_ref, c_ref, sa_ref, sb_ref, mu_ref, prev_ref,
                   q_ref, kf_ref, kb_ref, vf_ref, vb_ref, qi_ref, kiwi_ref, kib_ref, sh_ref, last_ref,
                   gate_ref, carry_ref):
    tm = x_ref.shape[0]
    xn = _rms(x_ref[...], g_ref[...]).astype(BF16)
    c, sa, sb = c_ref[...], sa_ref[...], sb_ref[...]

    def proj(c0, c1):
        return _dot(xn, w_ref[:, c0:c1])

    def rope(z):
        return jnp.concatenate(
            [_rope_slab(z[:, s:s + LANES], c, sa, sb) for s in range(0, z.shape[1], LANES)], axis=1)

    q = rope(proj(0, A_WIDTH)) * (HEAD_DIM ** -0.5)
    q_ref[...] = q.astype(BF16)
    k = rope(proj(A_WIDTH, 2 * A_WIDTH))
    kf_ref[...] = k
    kb_ref[...] = k.astype(BF16)
    v = proj(2 * A_WIDTH, QKV_END)
    vf_ref[...] = v
    vb_ref[...] = v.astype(BF16)
    qi_ref[...] = rope(proj(QKV_END, QI_END)).astype(BF16)
    kiwi = proj(QI_END, KIWI_END)
    lane = lax.broadcasted_iota(jnp.int32, kiwi.shape, 1)
    kiwi = jnp.where(lane < IDX_DIM, _rope_slab(kiwi, c, sa, sb), kiwi * IDX_W_SCALE)
    kiwi_ref[...] = kiwi
    kib_ref[...] = kiwi[:, :IDX_DIM].astype(BF16)

    u = proj(KIWI_END, SHIFT_END)
    if seq_mode:
        @pl.when(pl.program_id(0) == 0)
        def _():
            carry_ref[...] = prev_ref[...]
        row = lax.broadcasted_iota(jnp.int32, u.shape, 0)
        u_prev = jnp.where(row == 0, carry_ref[...], pltpu.roll(u, 1, 0))
        carry_ref[...] = u[tm - 1:tm, :]
    else:
        u_prev = prev_ref[...]
    sh_ref[...] = u + (u_prev - u) * mu_ref[...]
    last_ref[...] = u[tm - 1:tm, :] if seq_mode else u
    gate_ref[...] = jax.nn.sigmoid(proj(SHIFT_END, D_IN_PACKED))


def _inproj(x, norm_g, w_packed, rope_c, rope_sa, rope_sb, mu, prev, seq_mode, tm):
    t = x.shape[0]
    assert t % tm == 0
    row = lambda w: pl.BlockSpec((tm, w), lambda i: (i, 0))
    const = lambda a: pl.BlockSpec(a.shape, lambda i: (0, 0))
    prev_spec = const(prev) if seq_mode else row(SHIFT_W)
    out_shape = (
        jax.ShapeDtypeStruct((t, A_WIDTH), BF16),
        jax.ShapeDtypeStruct((t, A_WIDTH), F32), jax.ShapeDtypeStruct((t, A_WIDTH), BF16),
        jax.ShapeDtypeStruct((t, A_WIDTH), F32), jax.ShapeDtypeStruct((t, A_WIDTH), BF16),
        jax.ShapeDtypeStruct((t, IDX_HEADS * IDX_DIM), BF16),
        jax.ShapeDtypeStruct((t, LANES), F32),
        jax.ShapeDtypeStruct((t, IDX_DIM), BF16),
        jax.ShapeDtypeStruct((t, SHIFT_W), F32),
        jax.ShapeDtypeStruct((1 if seq_mode else t, SHIFT_W), F32),
        jax.ShapeDtypeStruct((t, 2 * D_MODEL), F32),
    )
    out_specs = (row(A_WIDTH), row(A_WIDTH), row(A_WIDTH), row(A_WIDTH), row(A_WIDTH),
                 row(IDX_HEADS * IDX_DIM), row(LANES), row(IDX_DIM), row(SHIFT_W),
                 pl.BlockSpec((1, SHIFT_W), lambda i: (0, 0)) if seq_mode else row(SHIFT_W), row(2 * D_MODEL))
    return pl.pallas_call(
        functools.partial(_inproj_kernel, seq_mode),
        grid=(t // tm,),
        in_specs=[row(D_MODEL), const(norm_g), const(w_packed), row(LANES), row(LANES), row(LANES),
                  const(mu), prev_spec],
        out_specs=out_specs, out_shape=out_shape,
        scratch_shapes=[pltpu.VMEM((1, SHIFT_W), F32)],
        compiler_params=_cparams(("arbitrary",)),
        name="inproj_seq" if seq_mode else "inproj_batch",
    )(x, norm_g, w_packed, rope_c, rope_sa, rope_sb, mu, prev)


def _rope_tables(pos):
    half = ROT_DIM // 2
    inv = ROPE_THETA ** (-jnp.arange(half, dtype=F32) / half)
    ang = pos.astype(F32)[:, None] * inv[None, :]
    cos, sin = jnp.cos(ang), jnp.sin(ang)
    t = pos.shape[0]
    pad = jnp.zeros((t, HEAD_DIM - ROT_DIM), F32)
    zero = jnp.zeros((t, half), F32)
    c = jnp.concatenate([cos, cos, pad + 1.0], 1)
    sa = jnp.concatenate([zero, sin, pad], 1)
    sb = jnp.concatenate([-sin, zero, pad], 1)
    rep = lambda a: jnp.concatenate([a] * (LANES // HEAD_DIM), 1)
    return rep(c), rep(sa), rep(sb)


def _pack_w_in(w_in):
    lead = QI_END + IDX_DIM + IDX_HEADS
    pad = jnp.zeros((D_MODEL, KIWI_END - lead), w_in.dtype)
    return jnp.concatenate([w_in[:, :lead], pad, w_in[:, lead:]], 1).astype(BF16)


DSA_TQ = 256
DSA_TK = 512
THR_ROWS = 128
THR_CHEAP_STEPS = 3


def _causal_pairs(t, tq, tk):
    qi, kj, last = [], [], []
    for i in range(t // tq):
        nk = (i * tq + tq - 1) // tk + 1
        for j in range(nk):
            qi.append(i), kj.append(j), last.append(int(j == nk - 1))
    return (jnp.asarray(qi, jnp.int32), jnp.asarray(kj, jnp.int32), jnp.asarray(last, jnp.int32))


def _index_scores(qi, ki, wi, q0, k0):
    tq, tk = qi.shape[0], ki.shape[0]
    s = None
    for h in range(IDX_HEADS):
        sh = jnp.maximum(_dot_t(qi[:, h * IDX_DIM:(h + 1) * IDX_DIM], ki), 0.0)
        sh = sh * wi[:, IDX_DIM + h:IDX_DIM + h + 1]
        s = sh if s is None else s + sh
    qpos = q0 + lax.broadcasted_iota(jnp.int32, (tq, tk), 0)
    kpos = k0 + lax.broadcasted_iota(jnp.int32, (tq, tk), 1)
    return jnp.where(kpos <= qpos, s, -jnp.inf)


def _kth_largest_rows(sc_ref, r0, ncol, kf):
    rows = pl.ds(r0, THR_ROWS)
    shape = (THR_ROWS, LANES)

    def col_reduce(fn, init, combine):
        def body(c, acc):
            blk = sc_ref[rows, pl.ds(pl.multiple_of(c * LANES, LANES), LANES)]
            return combine(acc, fn(blk))
        return lax.fori_loop(0, ncol, body, jnp.full(shape, init, F32))

    def count_gt(m):
        mb = jnp.broadcast_to(m, shape)
        acc = col_reduce(lambda b: jnp.where(b > mb, 1.0, 0.0), 0.0, jnp.add)
        return jnp.sum(acc, axis=-1, keepdims=True)

    def snap(lx, hi):
        lb, hb = jnp.broadcast_to(lx, shape), jnp.broadcast_to(hi, shape)
        lo_acc = col_reduce(lambda b: jnp.where(b > lb, b, jnp.inf), jnp.inf, jnp.minimum)
        hi_acc = col_reduce(lambda b: jnp.where(b <= hb, b, -jnp.inf), -jnp.inf, jnp.maximum)
        return jnp.min(lo_acc, axis=-1, keepdims=True), jnp.max(hi_acc, axis=-1, keepdims=True)

    def pivot_step(lx, hi, m):
        up = count_gt(m) >= kf
        return jnp.where(up, m, lx), jnp.where(up, hi, m)

    def round_(state):
        lx, hi, _, _ = state
        d_lo, d_hi = snap(lx, hi)
        m = 0.5 * d_lo + 0.5 * d_hi
        m = jnp.where(m < d_hi, m, d_lo)
        lx, hi = pivot_step(lx, hi, m)
        a, b = jnp.maximum(lx, d_lo), jnp.minimum(hi, d_hi)
        for _ in range(THR_CHEAP_STEPS):
            m = 0.5 * a + 0.5 * b
            up = count_gt(m) >= kf
            lx, hi = jnp.where(up, m, lx), jnp.where(up, hi, m)
            a, b = jnp.where(up, m, a), jnp.where(up, b, m)
        open_rows = jnp.max(jnp.where(d_lo < d_hi, 1.0, 0.0))
        return lx, hi, d_lo, open_rows

    init = (jnp.full((THR_ROWS, 1), -jnp.inf, F32), jnp.full((THR_ROWS, 1), jnp.inf, F32),
            jnp.zeros((THR_ROWS, 1), F32), jnp.float32(1.0))
    lx, hi, thr, _ = lax.while_loop(lambda s: s[3] > 0.0, round_, init)
    tb = jnp.broadcast_to(thr, shape)
    n_gt = jnp.sum(col_reduce(lambda b: jnp.where(b > tb, 1.0, 0.0), 0.0, jnp.add), axis=-1, keepdims=True)
    n_ge = jnp.sum(col_reduce(lambda b: jnp.where(b >= tb, 1.0, 0.0), 0.0, jnp.add), axis=-1, keepdims=True)
    return thr, n_gt, n_ge


def _tie_cutoff_rows(sc_ref, r0, ncol, thr, need):
    rows = pl.ds(r0, THR_ROWS)
    shape = (THR_ROWS, LANES)
    tb = jnp.broadcast_to(thr, shape)
    lane = lax.broadcasted_iota(jnp.int32, shape, 1)

    def count_le(j):
        jb = jnp.broadcast_to(j, shape)

        def body(c, acc):
            blk = sc_ref[rows, pl.ds(pl.multiple_of(c * LANES, LANES), LANES)]
            hit = jnp.logical_and(blk == tb, lane + c * LANES <= jb)
            return acc + jnp.where(hit, 1.0, 0.0)
        return jnp.sum(lax.fori_loop(0, ncol, body, jnp.zeros(shape, F32)), axis=-1, keepdims=True)

    def step(_, lh):
        lo, hi = lh
        mid = (lo + hi) // 2
        ok = count_le(mid) >= need
        return jnp.where(ok, lo, mid), jnp.where(ok, mid, hi)

    lo = jnp.full((THR_ROWS, 1), -1, jnp.int32)
    hi = jnp.full((THR_ROWS, 1), 1, jnp.int32) * (ncol * LANES - 1)
    n_steps = int(np.ceil(np.log2(sc_ref.shape[1] + 1))) + 1
    _, hi = lax.fori_loop(0, n_steps, step, (lo, hi))
    return hi


def _dsa_thr_kernel(qi_blk, kj_blk, last_blk, qi_ref, wi_ref, ki_ref, thr_ref, cut_ref, sc_ref):
    p = pl.program_id(0)
    i, j = qi_blk[p], kj_blk[p]
    tq, tk = qi_ref.shape[0], ki_ref.shape[0]
    s = _index_scores(qi_ref[...], ki_ref[...], wi_ref[...], i * tq, j * tk)
    sc_ref[:, pl.ds(pl.multiple_of(j * tk, tk), tk)] = s

    @pl.when(last_blk[p] == 1)
    def _():
        ncol = (j + 1) * (tk // LANES)
        for g in range(tq // THR_ROWS):
            r0 = g * THR_ROWS
            qpos = i * tq + r0 + lax.broadcasted_iota(jnp.int32, (THR_ROWS, 1), 0)
            kf = jnp.minimum(qpos + 1, TOPK_MAX).astype(F32)
            thr, n_gt, n_ge = _kth_largest_rows(sc_ref, r0, ncol, kf)
            thr_ref[r0:r0 + THR_ROWS, :] = jnp.broadcast_to(thr, (THR_ROWS, LANES))
            cut_ref[r0:r0 + THR_ROWS, :] = jnp.full((THR_ROWS, LANES), sc_ref.shape[1], jnp.int32)

            @pl.when(jnp.max(n_ge - kf) > 0.0)
            def _():
                cut = _tie_cutoff_rows(sc_ref, r0, ncol, thr, kf - n_gt)
                cut = jnp.where(n_ge > kf, cut, sc_ref.shape[1])
                cut_ref[r0:r0 + THR_ROWS, :] = jnp.broadcast_to(cut, (THR_ROWS, LANES))


def _dsa_attn_kernel(qi_blk, kj_blk, last_blk, q_ref, qi_ref, wi_ref, thr_ref, cut_ref, ki_ref, k_ref, v_ref,
                     o_ref, m_ref, l_ref, acc_ref):
    p = pl.program_id(0)
    i, j = qi_blk[p], kj_blk[p]
    tq, tk = q_ref.shape[0], k_ref.shape[0]

    @pl.when(j == 0)
    def _():
        m_ref[...] = jnp.full(m_ref.shape, NEG_BIG, F32)
        l_ref[...] = jnp.zeros(l_ref.shape, F32)
        acc_ref[...] = jnp.zeros(acc_ref.shape, F32)

    s = _index_scores(qi_ref[...], ki_ref[...], wi_ref[...], i * tq, j * tk)
    thr = thr_ref[:, 0:1]
    kpos = j * tk + lax.broadcasted_iota(jnp.int32, (tq, tk), 1)
    sel = jnp.logical_or(s > thr, jnp.logical_and(s == thr, kpos <= cut_ref[:, 0:1]))
    for h in range(A_HEADS):
        hs = slice(h * HEAD_DIM, (h + 1) * HEAD_DIM)
        logits = jnp.where(sel, _dot_t(q_ref[:, hs], k_ref[:, hs]), NEG_BIG)
        m_prev = m_ref[h]
        m_new = jnp.maximum(m_prev, jnp.max(logits, axis=-1, keepdims=True))
        alpha = jnp.exp(m_prev - m_new)
        pr = jnp.where(sel, jnp.exp(logits - m_new[:, 0:1]), 0.0)
        l_ref[h] = alpha * l_ref[h] + jnp.sum(pr, axis=-1, keepdims=True)
        acc_ref[h] = alpha[:, 0:HEAD_DIM] * acc_ref[h] + _dot(pr.astype(BF16), v_ref[:, hs])
        m_ref[h] = m_new

    @pl.when(last_blk[p] == 1)
    def _():
        o_ref[...] = jnp.concatenate(
            [acc_ref[h] / l_ref[h][:, 0:HEAD_DIM] for h in range(A_HEADS)], axis=1)


def _dsa_prompt(q, k, v, qi, kiwi, ki):
    t = q.shape[0]
    tq, tk = min(DSA_TQ, t), min(DSA_TK, t)
    assert t % tq == 0 and t % tk == 0 and tq % THR_ROWS == 0
    pairs = _causal_pairs(t, tq, tk)
    n_pairs = pairs[0].shape[0]
    qrow = lambda w: pl.BlockSpec((tq, w), lambda p, qb, kb, lb: (qb[p], 0))
    krow = lambda w: pl.BlockSpec((tk, w), lambda p, qb, kb, lb: (kb[p], 0))
    thr, cut = pl.pallas_call(
        _dsa_thr_kernel,
        grid_spec=pltpu.PrefetchScalarGridSpec(
            num_scalar_prefetch=3, grid=(n_pairs,),
            in_specs=[qrow(IDX_HEADS * IDX_DIM), qrow(LANES), krow(IDX_DIM)],
            out_specs=(qrow(LANES), qrow(LANES)),
            scratch_shapes=[pltpu.VMEM((tq, t), F32)]),
        out_shape=(jax.ShapeDtypeStruct((t, LANES), F32), jax.ShapeDtypeStruct((t, LANES), jnp.int32)),
        compiler_params=_cparams(("arbitrary",)), name="dsa_prompt_threshold",
    )(*pairs, qi, kiwi, ki)
    return pl.pallas_call(
        _dsa_attn_kernel,
        grid_spec=pltpu.PrefetchScalarGridSpec(
            num_scalar_prefetch=3, grid=(n_pairs,),
            in_specs=[qrow(A_WIDTH), qrow(IDX_HEADS * IDX_DIM), qrow(LANES), qrow(LANES), qrow(LANES),
                      krow(IDX_DIM), krow(A_WIDTH), krow(A_WIDTH)],
            out_specs=qrow(A_WIDTH),
            scratch_shapes=[pltpu.VMEM((A_HEADS, tq, LANES), F32), pltpu.VMEM((A_HEADS, tq, LANES), F32),
                            pltpu.VMEM((A_HEADS, tq, HEAD_DIM), F32)]),
        out_shape=jax.ShapeDtypeStruct((t, A_WIDTH), F32),
        compiler_params=_cparams(("arbitrary",)), name="dsa_prompt_attention",
    )(*pairs, q, qi, kiwi, thr, cut, ki, k, v)


IDX_ROWS = 16


def _dsa_sample_scores_kernel(pt_ref, qi_ref, wi_ref, kin_ref, cache_ref, o_ref, buf_ref, sem_ref):
    b, nb = pl.program_id(0), pl.num_programs(0)
    n_pages = pt_ref.shape[1]
    past = n_pages * PAGE_SIZE

    def issue(bb, slot):
        for p in range(n_pages):
            pltpu.make_async_copy(cache_ref.at[0, pt_ref[bb, p]],
                                  buf_ref.at[slot, pl.ds(p * PAGE_SIZE, PAGE_SIZE)], sem_ref.at[slot]).start()

    @pl.when(b == 0)
    def _():
        issue(0, 0)

    @pl.when(b + 1 < nb)
    def _():
        issue(b + 1, (b + 1) % 2)

    slot = b % 2
    pltpu.make_async_copy(buf_ref.at[slot], buf_ref.at[slot], sem_ref.at[slot]).wait()
    qi, wi = qi_ref[0], wi_ref[0]
    s = jnp.maximum(_dot_t(qi, buf_ref[slot].astype(BF16)), 0.0) * wi
    o_ref[0, :, 0:past] = jnp.sum(s, axis=0, keepdims=True)
    kin = kin_ref[0].astype(F32)
    s_new = jnp.maximum(jnp.sum(qi.astype(F32) * kin, axis=-1, keepdims=True), 0.0) * wi
    s_new = jnp.sum(s_new, axis=0, keepdims=True)
    lane = lax.broadcasted_iota(jnp.int32, (1, LANES), 1)
    o_ref[0, :, past:past + LANES] = jnp.where(lane == 0, s_new, -jnp.inf)


def _dsa_sample_scores(page_table, qi3, wi3, ki_new, cache_idx_k):
    bsz, n_pages = page_table.shape
    past = n_pages * PAGE_SIZE
    blk = lambda a: pl.BlockSpec((1,) + a.shape[1:], lambda b, pt: (b, 0, 0))
    return pl.pallas_call(
        _dsa_sample_scores_kernel,
        grid_spec=pltpu.PrefetchScalarGridSpec(
            num_scalar_prefetch=1, grid=(bsz,),
            in_specs=[blk(qi3), blk(wi3), blk(ki_new), pl.BlockSpec(memory_space=pl.ANY)],
            out_specs=pl.BlockSpec((1, 1, past + LANES), lambda b, pt: (b, 0, 0)),
            scratch_shapes=[pltpu.VMEM((2, past, IDX_DIM), F32), pltpu.SemaphoreType.DMA((2,))]),
        out_shape=jax.ShapeDtypeStruct((bsz, 1, past + LANES), F32),
        compiler_params=_cparams(("arbitrary",)), name="dsa_sample_scores",
    )(page_table, qi3, wi3, ki_new, cache_idx_k)


def _topk_extract_kernel(k_sel, sc_ref, sel_ref, work_ref):
    bsz, width = sc_ref.shape
    ncol = width // LANES
    shape = (bsz, LANES)
    work_ref[...] = sc_ref[...]
    lane = lax.broadcasted_iota(jnp.int32, shape, 1)
    out_lane = lax.broadcasted_iota(jnp.int32, (bsz, k_sel), 1)

    def chunk(c):
        return work_ref[:, pl.ds(pl.multiple_of(c * LANES, LANES), LANES)]

    def body(r, sel):
        mx = lax.fori_loop(0, ncol, lambda c, a: jnp.maximum(a, chunk(c)), jnp.full(shape, -jnp.inf, F32))
        mxb = jnp.broadcast_to(jnp.max(mx, axis=-1, keepdims=True), shape)
        big = jnp.int32(width)
        ix = lax.fori_loop(0, ncol, lambda c, a: jnp.minimum(a, jnp.where(chunk(c) == mxb, lane + c * LANES, big)),
                           jnp.full(shape, width, jnp.int32))
        idx = jnp.min(ix, axis=-1, keepdims=True)
        ixb = jnp.broadcast_to(idx, shape)

        def knock(c, carry):
            work_ref[:, pl.ds(pl.multiple_of(c * LANES, LANES), LANES)] = jnp.where(
                lane + c * LANES == ixb, -jnp.inf, chunk(c))
            return carry
        lax.fori_loop(0, ncol, knock, 0)
        return jnp.where(out_lane == r, idx, sel)

    n_rounds = k_sel + 0 * pl.program_id(0)
    sel_ref[...] = lax.fori_loop(0, n_rounds, body, jnp.zeros((bsz, k_sel), jnp.int32))


def _topk_extract(scores, k_sel):
    bsz, width = scores.shape
    whole = lambda w: pl.BlockSpec((bsz, w), lambda i: (0, 0))
    return pl.pallas_call(
        functools.partial(_topk_extract_kernel, k_sel), grid=(1,),
        in_specs=[whole(width)], out_specs=whole(k_sel),
        out_shape=jax.ShapeDtypeStruct((bsz, k_sel), jnp.int32),
        scratch_shapes=[pltpu.VMEM((bsz, width), F32)],
        compiler_params=_cparams(("arbitrary",)), name="dsa_sample_topk",
    )(scores)


def _dsa_sample_attn_kernel(sel_ref, pt_ref, q_ref, kn_ref, vn_ref, selc_ref, ck_ref, cv_ref, o_ref,
                            kbuf, vbuf, sem_ref):
    b, nb = pl.program_id(0), pl.num_programs(0)
    k_sel = sel_ref.shape[1]
    past = pt_ref.shape[1] * PAGE_SIZE

    def issue(bb, slot):
        def one(r, carry):
            sp = jnp.minimum(sel_ref[bb, r], past - 1)
            page = pt_ref[bb, lax.shift_right_logical(sp, 7)]
            off = jnp.bitwise_and(sp, PAGE_SIZE - 1)
            pltpu.make_async_copy(ck_ref.at[0, page, off], kbuf.at[slot, r], sem_ref.at[0, slot]).start()
            pltpu.make_async_copy(cv_ref.at[0, page, off], vbuf.at[slot, r], sem_ref.at[1, slot]).start()
            return carry
        lax.fori_loop(0, k_sel, one, 0)

    @pl.when(b == 0)
    def _():
        issue(0, 0)

    @pl.when(b + 1 < nb)
    def _():
        issue(b + 1, (b + 1) % 2)

    slot = b % 2
    pltpu.make_async_copy(kbuf.at[slot], kbuf.at[slot], sem_ref.at[0, slot]).wait()
    pltpu.make_async_copy(vbuf.at[slot], vbuf.at[slot], sem_ref.at[1, slot]).wait()
    is_new = selc_ref[0] >= past
    rnd = lambda z: z.astype(BF16).astype(F32)
    ks = rnd(jnp.where(is_new, kn_ref[...], kbuf[slot]))
    vs = rnd(jnp.where(is_new, vn_ref[...], vbuf[slot]))
    logits = jnp.sum(ks * q_ref[...].astype(F32), axis=-1, keepdims=True)
    m = jnp.max(logits, axis=0, keepdims=True)
    p = jnp.exp(logits - m)
    p = rnd(p / jnp.sum(p, axis=0, keepdims=True))
    o_ref[...] = jnp.sum(p * vs, axis=0, keepdims=True)


def _dsa_sample_attn(sel, page_table, q3, k_new3, v_new3, cache_k, cache_v):
    bsz, k_sel = sel.shape
    blk = lambda: pl.BlockSpec((1, A_HEADS, HEAD_DIM), lambda b, s, pt: (b, 0, 0))
    return pl.pallas_call(
        _dsa_sample_attn_kernel,
        grid_spec=pltpu.PrefetchScalarGridSpec(
            num_scalar_prefetch=2, grid=(bsz,),
            in_specs=[blk(), blk(), blk(), pl.BlockSpec((1, k_sel, 1, 1), lambda b, s, pt: (b, 0, 0, 0)),
                      pl.BlockSpec(memory_space=pl.ANY), pl.BlockSpec(memory_space=pl.ANY)],
            out_specs=blk(),
            scratch_shapes=[pltpu.VMEM((2, k_sel, A_HEADS, HEAD_DIM), F32),
                            pltpu.VMEM((2, k_sel, A_HEADS, HEAD_DIM), F32), pltpu.SemaphoreType.DMA((2, 2))]),
        out_shape=jax.ShapeDtypeStruct((bsz, A_HEADS, HEAD_DIM), F32),
        compiler_params=_cparams(("arbitrary",)), name="dsa_sample_attention",
    )(sel, page_table, q3, k_new3, v_new3, sel.reshape(bsz, k_sel, 1, 1), cache_k, cache_v)


def _head_ones():
    h = np.arange(B_WIDTH) // B_HEAD_DIM
    return jnp.asarray((h[:, None] == h[None, :]).astype(np.float32))


def _rwkv_pre_kernel(sh_ref, w0_ref, wup_ref, a0_ref, aup_ref, gup_ref, kk_ref_, ka_ref, g1_ref,
                     ld_ref, kk_ref, kb_ref, kp_ref, g_ref):
    k = sh_ref[:, B_WIDTH:2 * B_WIDTH]
    o = 3 * B_WIDTH
    xw = sh_ref[:, o:o + DECAY_LORA]
    xa = sh_ref[:, o + DECAY_LORA:o + DECAY_LORA + AAA_LORA]
    xg = sh_ref[:, o + DECAY_LORA + AAA_LORA:SHIFT_W]
    w = w0_ref[...] + _dotx(jnp.tanh(xw), wup_ref[...])
    logw = -jax.nn.softplus(-w) - 0.5
    ld_ref[...] = -jnp.exp(logw)
    a = jax.nn.sigmoid(a0_ref[...] + _dotx(xa, aup_ref[...]))
    g_ref[...] = _dotx(jax.nn.sigmoid(xg), gup_ref[...])
    kk = k * kk_ref_[...]
    kk = kk / jnp.maximum(jnp.sqrt(_dotx(kk * kk, g1_ref[...])), 1e-12)
    kk_ref[...] = kk
    kb_ref[...] = kk * a
    kp_ref[...] = k * (1.0 + (a - 1.0) * ka_ref[...])


def _rwkv_pre(shifted, w0, w_lora_up, a0, a_lora_up, g_lora_up, k_k, k_a, tm):
    t = shifted.shape[0]
    row = lambda w: pl.BlockSpec((tm, w), lambda i: (i, 0))
    const = lambda a: pl.BlockSpec(a.shape, lambda i: (0, 0))
    g1 = _head_ones()
    args = (shifted, w0, w_lora_up, a0, a_lora_up, g_lora_up, k_k, k_a, g1)
    out = jax.ShapeDtypeStruct((t, B_WIDTH), F32)
    return pl.pallas_call(
        _rwkv_pre_kernel, grid=(t // tm,),
        in_specs=[row(SHIFT_W)] + [const(a) for a in args[1:]],
        out_specs=(row(B_WIDTH),) * 5, out_shape=(out,) * 5,
        compiler_params=_cparams(("arbitrary",)), name="rwkv_prep",
    )(*args)


def _rwkv_seq_kernel(r_ref, v_ref, ld_ref, kk_ref, kb_ref, kp_ref, y_ref, so_ref, st_ref):
    c = CHUNK

    @pl.when(pl.program_id(0) == 0)
    def _():
        st_ref[...] = jnp.zeros(st_ref.shape, F32)

    ri = lax.broadcasted_iota(jnp.int32, (c, c), 0)
    ci = lax.broadcasted_iota(jnp.int32, (c, c), 1)
    incl, strict, eye = ri >= ci, ri > ci, ri == ci
    ld = ld_ref[...]
    cum = _dotx(jnp.where(incl, 1.0, 0.0), ld)
    cum_c = cum[c - 1:c, :]
    e_neg, e_rem = jnp.exp(-cum), jnp.exp(cum_c - cum)
    kk, kb, kp = kk_ref[...], kb_ref[...], kp_ref[...]
    a_all = -kk * jnp.exp(cum - ld)
    r_all = r_ref[...] * jnp.exp(cum)
    bt_all, kt_all = kb * e_neg, kp * e_neg
    bh_all, kh_all = kb * e_rem, kp * e_rem
    pc = jnp.exp(cum_c)
    v_all = v_ref[...]
    for h in range(B_HEADS):
        hs = slice(h * B_HEAD_DIM, (h + 1) * B_HEAD_DIM)
        n = B_HEAD_DIM
        a_h, v_h = a_all[:, hs], v_all[:, hs]
        p = _dotx_t(jnp.concatenate([a_h, r_all[:, hs]], 0), jnp.concatenate([bt_all[:, hs], kt_all[:, hs]], 0))
        l_ab = jnp.where(strict, p[:c, :c], 0.0)
        l_ak = jnp.where(strict, p[:c, c:], 0.0)
        m_rb = jnp.where(incl, p[c:, :c], 0.0)
        m_rk = jnp.where(incl, p[c:, c:], 0.0)
        x = jnp.concatenate([a_h, _dotx(l_ak, v_h)], 1)
        lp = l_ab
        n_dbl = int(np.log2(c))
        for it in range(n_dbl):
            x = x + _dotx(lp, x)
            if it + 1 < n_dbl:
                lp = _dotx(lp, lp)
        bw = _dotx(bh_all[:, hs].T, x)
        mw = _dotx(m_rb, x)
        m_c = jnp.where(eye, jnp.broadcast_to(pc[:, hs], (n, n)), 0.0) + bw[:, :n]
        n_c = bw[:, n:] + _dotx(kh_all[:, hs].T, v_h)
        g_c = r_all[:, hs] + mw[:, :n]
        y0 = mw[:, n:] + _dotx(m_rk, v_h)
        s0 = st_ref[h]
        y_ref[:, hs] = _dotx(g_c, s0) + y0
        st_ref[h] = _dotx(m_c, s0) + n_c
    so_ref[...] = st_ref[...]


def _rwkv_seq(shifted, ld, kk, kb, kp):
    t = shifted.shape[0]
    assert t % CHUNK == 0
    col = lambda j: pl.BlockSpec((CHUNK, B_WIDTH), lambda i: (i, j))
    state = jax.ShapeDtypeStruct((B_HEADS, B_HEAD_DIM, B_HEAD_DIM), F32)
    return pl.pallas_call(
        _rwkv_seq_kernel, grid=(t // CHUNK,),
        in_specs=[col(0), col(2), col(0), col(0), col(0), col(0)],
        out_specs=(col(0), pl.BlockSpec(state.shape, lambda i: (0, 0, 0))),
        out_shape=(jax.ShapeDtypeStruct((t, B_WIDTH), F32), state),
        scratch_shapes=[pltpu.VMEM(state.shape, F32)],
        compiler_params=_cparams(("arbitrary",)), name="rwkv_chunked_scan",
    )(shifted, shifted, ld, kk, kb, kp)


RWKV_STEP_ROWS = 8


def _rwkv_step_kernel(s_ref, r_ref, ld_ref, kk_ref, kb_ref, kp_ref, v_ref, so_ref, y_ref):
    def one(b, carry):
        row = pl.ds(b, 1)
        r, w, kk, kb, kp = r_ref[row, :], jnp.exp(ld_ref[row, :]), kk_ref[row, :], kb_ref[row, :], kp_ref[row, :]
        for h in range(B_HEADS):
            hs = slice(h * B_HEAD_DIM, (h + 1) * B_HEAD_DIM)
            s = s_ref[b, h]
            sa = -jnp.sum(s * kk[:, hs], axis=-1, keepdims=True)
            s2 = s * w[:, hs] + sa * kb[:, hs] + v_ref[b, h] * kp[:, hs]
            so_ref[b, h] = s2
            y_ref[b, h] = jnp.sum(s2 * r[:, hs], axis=-1, keepdims=True)
        return carry
    lax.fori_loop(0, s_ref.shape[0], one, 0)


def _rwkv_step(state, shifted, ld, kk, kb, kp, v_col):
    bsz = state.shape[0]
    nb = RWKV_STEP_ROWS
    assert bsz % nb == 0
    row = lambda: pl.BlockSpec((nb, B_WIDTH), lambda i: (i, 0))
    st = pl.BlockSpec((nb, B_HEADS, B_HEAD_DIM, B_HEAD_DIM), lambda i: (i, 0, 0, 0))
    colv = pl.BlockSpec((nb, B_HEADS, B_HEAD_DIM, 1), lambda i: (i, 0, 0, 0))
    return pl.pallas_call(
        _rwkv_step_kernel, grid=(bsz // nb,),
        in_specs=[st, row(), row(), row(), row(), row(), colv],
        out_specs=(st, colv),
        out_shape=(jax.ShapeDtypeStruct(state.shape, F32),
                   jax.ShapeDtypeStruct((bsz, B_HEADS, B_HEAD_DIM, 1), F32)),
        compiler_params=_cparams(("arbitrary",)), name="rwkv_single_step",
    )(state, shifted, ld, kk, kb, kp, v_col)


def _rwkv_post_kernel(y_ref, r_ref, v_ref, kp_ref, g_ref, rk_ref, gng_ref, gnb_ref, g1_ref, o_ref):
    g1 = g1_ref[...]
    inv_n = 1.0 / B_HEAD_DIM
    y = y_ref[...]
    d = y - _dotx(y, g1) * inv_n
    var = _dotx(d * d, g1) * inv_n
    yn = d * lax.rsqrt(var + GN_EPS) * gng_ref[...] + gnb_ref[...]
    bonus = _dotx(r_ref[...] * kp_ref[...] * rk_ref[...], g1) * v_ref[...]
    o_ref[...] = (yn + bonus) * g_ref[...]


def _rwkv_post(y, shifted, kp, g, r_k, gn_g, gn_b, tm):
    t = y.shape[0]
    col = lambda j: pl.BlockSpec((tm, B_WIDTH), lambda i: (i, j))
    const = lambda a: pl.BlockSpec(a.shape, lambda i: (0, 0))
    g1 = _head_ones()
    return pl.pallas_call(
        _rwkv_post_kernel, grid=(t // tm,),
        in_specs=[col(0), col(0), col(2), col(0), col(0), const(r_k), const(gn_g), const(gn_b), const(g1)],
        out_specs=col(0), out_shape=jax.ShapeDtypeStruct((t, B_WIDTH), F32),
        compiler_params=_cparams(("arbitrary",)), name="rwkv_output_norm",
    )(y, shifted, shifted, kp, g, r_k, gn_g, gn_b, g1)


def _outproj_kernel(x_ref, attn_ref, rw_ref, gate_ref, wa_ref, wb_ref, wo_ref, nf_ref, wr_ref, br_ref,
                    h_ref, hn_ref, comb_ref):
    ga, gb = gate_ref[:, :D_MODEL], gate_ref[:, D_MODEL:]
    merged = ga * _dot(attn_ref[...].astype(BF16), wa_ref[...]) + gb * _dot(rw_ref[...].astype(BF16), wb_ref[...])
    h = x_ref[...] + _dot(merged.astype(BF16), wo_ref[...])
    h_ref[...] = h
    hn = _rms(h, nf_ref[...])
    hn_ref[...] = hn.astype(BF16)
    logits = _dotx(hn, wr_ref[...]) + br_ref[...]
    lane = lax.broadcasted_iota(jnp.int32, logits.shape, 1)
    big = jnp.int32(LANES)
    first = lambda hit: jnp.min(jnp.where(hit, lane, big), axis=-1, keepdims=True)
    is_g = lane < N_GROUPS
    gl = jnp.where(is_g, logits, -jnp.inf)
    gmax = jnp.max(gl, axis=-1, keepdims=True)
    p_group = 1.0 / jnp.sum(jnp.exp(gl - gmax), axis=-1, keepdims=True)
    g_sel = first(gl == gmax)
    e0 = N_GROUPS + g_sel * EXPERTS_PER_GROUP
    el = jnp.where(jnp.logical_and(lane >= e0, lane < e0 + EXPERTS_PER_GROUP), logits, -jnp.inf)
    v1 = jnp.max(el, axis=-1, keepdims=True)
    i1 = first(el == v1)
    el2 = jnp.where(lane == i1, -jnp.inf, el)
    v2 = jnp.max(el2, axis=-1, keepdims=True)
    i2 = first(el2 == v2)
    e21 = jnp.exp(v2 - v1)
    w1 = p_group / (1.0 + e21)
    w2 = p_group * e21 / (1.0 + e21)
    comb_ref[...] = jnp.where(lane == i1 - N_GROUPS, w1, 0.0) + jnp.where(lane == i2 - N_GROUPS, w2, 0.0)


def _outproj(x, attn, rw, gates, wa, wb, wo, norm_ffn, w_router, b_router, tm):
    t = x.shape[0]
    row = lambda w: pl.BlockSpec((tm, w), lambda i: (i, 0))
    const = lambda a: pl.BlockSpec(a.shape, lambda i: (0, 0))
    return pl.pallas_call(
        _outproj_kernel, grid=(t // tm,),
        in_specs=[row(D_MODEL), row(A_WIDTH), row(B_WIDTH), row(2 * D_MODEL), const(wa), const(wb), const(wo),
                  const(norm_ffn), const(w_router), const(b_router)],
        out_specs=(row(D_MODEL), row(D_MODEL), row(LANES)),
        out_shape=(jax.ShapeDtypeStruct((t, D_MODEL), F32), jax.ShapeDtypeStruct((t, D_MODEL), BF16),
                   jax.ShapeDtypeStruct((t, LANES), F32)),
        compiler_params=_cparams(("arbitrary",)), name="outproj_router",
    )(x, attn, rw, gates, wa, wb, wo, norm_ffn, w_router, b_router)


def _moe_kernel(h_ref, hn_ref, comb_ref, wg_ref, wu_ref, wd_ref, nfin_ref, o_ref, acc_ref):
    e = pl.program_id(1)
    hn = hn_ref[...]
    act = jax.nn.silu(_dot(hn, wg_ref[0])) * _dot(hn, wu_ref[0])
    y = _dot(act.astype(BF16), wd_ref[0])
    lane = lax.broadcasted_iota(jnp.int32, comb_ref.shape, 1)
    c_e = jnp.sum(jnp.where(lane == e, comb_ref[...], 0.0), axis=-1, keepdims=True)

    @pl.when(e == 0)
    def _():
        acc_ref[...] = c_e * y

    @pl.when(e > 0)
    def _():
        acc_ref[...] += c_e * y

    @pl.when(e == pl.num_programs(1) - 1)
    def _():
        o_ref[...] = _rms(h_ref[...] + acc_ref[...], nfin_ref[...])


def _moe(h, hn, comb, wg, wu, wd, norm_final, tm):
    t = h.shape[0]
    row = lambda w: pl.BlockSpec((tm, w), lambda i, e: (i, 0))
    return pl.pallas_call(
        _moe_kernel, grid=(t // tm, N_EXPERTS),
        in_specs=[row(D_MODEL), row(D_MODEL), row(LANES),
                  pl.BlockSpec((1, D_MODEL, D_EXPERT), lambda i, e: (e, 0, 0)),
                  pl.BlockSpec((1, D_MODEL, D_EXPERT), lambda i, e: (e, 0, 0)),
                  pl.BlockSpec((1, D_EXPERT, D_MODEL), lambda i, e: (e, 0, 0)),
                  pl.BlockSpec(norm_final.shape, lambda i, e: (0, 0))],
        out_specs=row(D_MODEL), out_shape=jax.ShapeDtypeStruct((t, D_MODEL), F32),
        scratch_shapes=[pltpu.VMEM((tm, D_MODEL), F32)],
        compiler_params=_cparams(("arbitrary", "arbitrary")), name="moe_experts",
    )(h, hn, comb, wg, wu, wd, norm_final)


def _tile(t, pref):
    return min(pref, t)


def _mixer_tail(x, attn, shifted, gates, state_step, weights, ffn):
    (w0, w_lora_up, a0, a_lora_up, g_lora_up, k_k, k_a, r_k, gn_g, gn_b, wa, wb, wo) = weights
    (norm_ffn, w_router, b_router, wg, wu, wd, norm_final) = ffn
    t = x.shape[0]
    tm = _tile(t, 256)
    ld, kk, kb, kp, g = _rwkv_pre(shifted, w0, w_lora_up, a0, a_lora_up, g_lora_up, k_k, k_a, tm)
    if state_step is None:
        y, st = _rwkv_seq(shifted, ld, kk, kb, kp)
        wkv = jnp.swapaxes(st, 1, 2)[None]
    else:
        v_col = shifted[:, 2 * B_WIDTH:3 * B_WIDTH].reshape(t, B_HEADS, B_HEAD_DIM, 1)
        wkv, y_col = _rwkv_step(state_step, shifted, ld, kk, kb, kp, v_col)
        y = y_col.reshape(t, B_WIDTH)
    rw = _rwkv_post(y, shifted, kp, g, r_k, gn_g, gn_b, tm)
    h, hn, comb = _outproj(x, attn, rw, gates, wa, wb, wo, norm_ffn, w_router, b_router, tm)
    out = _moe(h, hn, comb, wg, wu, wd, norm_final, _tile(t, 1024))
    return out, wkv


def kernel(x_prompt, x_sample, cache_k, cache_v, cache_idx_k, state_wkv, state_shift, page_table, norm_mix, w_in, mu_shift, w0, w_lora_up, a0, a_lora_up, g_lora_up, k_k, k_a, r_k, gn_g, gn_b, w_branch_a, w_branch_b, w_out, norm_ffn, w_router_group, b_router_group, w_router_expert, b_router_expert, w_gate, w_up, w_down, norm_final):
    assert w_in.shape[0] == 1, "single-layer kernel"
    bp, tp, _ = x_prompt.shape
    bs, ts, _ = x_sample.shape
    assert bp == 1 and ts == 1
    n_pages = page_table.shape[1]
    past = n_pages * PAGE_SIZE
    row2 = lambda a: a.reshape(1, -1)

    w_packed = _pack_w_in(w_in[0])
    mix_w = (row2(w0[0]), w_lora_up[0], row2(a0[0]), a_lora_up[0], g_lora_up[0], row2(k_k[0]), row2(k_a[0]),
             row2(r_k[0]), row2(gn_g[0]), row2(gn_b[0]),
             w_branch_a[0].astype(BF16), w_branch_b[0].astype(BF16), w_out[0].astype(BF16))
    pad_r = jnp.zeros((D_MODEL, LANES - N_GROUPS - N_EXPERTS), F32)
    w_router = jnp.concatenate([w_router_group[0], w_router_expert[0], pad_r], 1)
    b_router = jnp.concatenate([b_router_group[0], b_router_expert[0], pad_r[0]])[None]
    ffn_w = (row2(norm_ffn[0]), w_router, b_router, w_gate[0].astype(BF16), w_up[0].astype(BF16),
             w_down[0].astype(BF16), row2(norm_final))
    g_mix, mu = row2(norm_mix[0]), row2(mu_shift[0])

    xp = x_prompt[0]
    tabs = _rope_tables(jnp.arange(tp))
    (q, kf, kb16, vf, vb16, qi, kiwi, ki16, shifted, shift_last, gates) = _inproj(
        xp, g_mix, w_packed, *tabs, mu, jnp.zeros((1, SHIFT_W), F32), True, _tile(tp, 256))
    attn = _dsa_prompt(q, kb16, vb16, qi, kiwi, ki16)
    y_p, wkv_p = _mixer_tail(xp, attn, shifted, gates, None, mix_w, ffn_w)

    xs = x_sample[:, 0]
    tabs_s = _rope_tables(jnp.full((bs,), past, jnp.int32))
    (q_s, kf_s, _, vf_s, _, qi_s, kiwi_s, ki16_s, shifted_s, shift_raw_s, gates_s) = _inproj(
        xs, g_mix, w_packed, *tabs_s, mu, state_shift[0], False, _tile(bs, 256))
    pad_h = IDX_ROWS - IDX_HEADS
    qi3 = jnp.pad(qi_s.reshape(bs, IDX_HEADS, IDX_DIM), ((0, 0), (0, pad_h), (0, 0)))
    wi3 = jnp.pad(kiwi_s[:, IDX_DIM:IDX_DIM + IDX_HEADS], ((0, 0), (0, pad_h)))[:, :, None]
    scores = _dsa_sample_scores(page_table, qi3, wi3, ki16_s[:, None, :], cache_idx_k)
    k_sel = min(TOPK_MAX, (past + ts) // 4)
    sel = _topk_extract(scores[:, 0], k_sel)
    head3 = lambda a: a.reshape(bs, A_HEADS, HEAD_DIM)
    attn_s = _dsa_sample_attn(sel, page_table, head3(q_s), head3(kf_s), head3(vf_s), cache_k, cache_v)
    y_s, wkv_s = _mixer_tail(xs, attn_s.reshape(bs, A_WIDTH), shifted_s, gates_s, state_wkv[0], mix_w, ffn_w)

    kv5 = lambda a, b, t: a.reshape(1, b, t, A_HEADS, HEAD_DIM)
    return (y_p[None], y_s[:, None],
            kv5(kf, 1, tp), kv5(vf, 1, tp), kiwi[:, :IDX_DIM].reshape(1, 1, tp, IDX_DIM),
            wkv_p[None], shift_last.reshape(1, 1, SHIFT_W),
            kv5(kf_s, bs, 1), kv5(vf_s, bs, 1), kiwi_s[:, :IDX_DIM].reshape(1, bs, 1, IDX_DIM),
            wkv_s[None], shift_raw_s[None])
```

```python
import functools

import jax
import jax.numpy as jnp
import numpy as np
from jax import lax
from jax.experimental import pallas as pl
from jax.experimental.pallas import tpu as pltpu

F32 = jnp.float32
BF16 = jnp.bfloat16

D_MODEL = 1024
PAGE_SIZE = 128
A_HEADS = 8
HEAD_DIM = 64
A_WIDTH = A_HEADS * HEAD_DIM
IDX_HEADS = 4
IDX_DIM = 64
TOPK_MAX = 256
ROT_DIM = HEAD_DIM // 4
ROPE_THETA = 500000.0
B_HEADS = 8
B_HEAD_DIM = 64
B_WIDTH = B_HEADS * B_HEAD_DIM
DECAY_LORA = 64
AAA_LORA = 64
GATE_LORA = 128
GN_EPS = B_HEAD_DIM * 1e-5
SHIFT_W = 3 * B_WIDTH + DECAY_LORA + AAA_LORA + GATE_LORA
N_GROUPS = 4
EXPERTS_PER_GROUP = 8
N_EXPERTS = N_GROUPS * EXPERTS_PER_GROUP
D_EXPERT = 512
RMS_EPS = 1e-6

LANES = 128
SUBLANES = 8
VMEM_LIMIT_BYTES = 56 * 1024 * 1024

QKV_END = 3 * A_WIDTH
QI_END = QKV_END + IDX_HEADS * IDX_DIM
KIWI_END = QI_END + LANES
SHIFT_END = KIWI_END + SHIFT_W
D_IN_PACKED = SHIFT_END + 2 * D_MODEL
IDX_W_SCALE = IDX_HEADS ** -0.5 * IDX_DIM ** -0.5
NEG_BIG = -1e30
CHUNK = 64


def _cparams(sem):
    return pltpu.CompilerParams(dimension_semantics=sem, vmem_limit_bytes=VMEM_LIMIT_BYTES)


def _dot(a, b):
    return jnp.dot(a, b, preferred_element_type=F32)


def _dot_t(a, b):
    return lax.dot_general(a, b, (((1,), (1,)), ((), ())), preferred_element_type=F32)


def _dotx(a, b):
    return jnp.dot(a, b, preferred_element_type=F32, precision=lax.Precision.HIGHEST)


def _dotx_t(a, b):
    return lax.dot_general(a, b, (((1,), (1,)), ((), ())), preferred_element_type=F32,
                           precision=lax.Precision.HIGHEST)


def _split2(a):
    hi = a.astype(BF16)
    return hi, (a - hi.astype(F32)).astype(BF16)


def _dot3(a, b):
    ah, al = _split2(a)
    bh, bl = _split2(b)
    return _dot(ah, bh) + (_dot(ah, bl) + _dot(al, bh))


def _dot3_t(a, b):
    ah, al = _split2(a)
    bh, bl = _split2(b)
    return _dot_t(ah, bh) + (_dot_t(ah, bl) + _dot_t(al, bh))


def _rms(x, g):
    return x * lax.rsqrt(jnp.mean(x * x, axis=-1, keepdims=True) + RMS_EPS) * g


def _rope_slab(x, c, sa, sb):
    return x * c + pltpu.roll(x, ROT_DIM // 2, 1) * sa + pltpu.roll(x, LANES - ROT_DIM // 2, 1) * sb


def _inproj_kernel(seq_mode, x_ref, g_ref, w_ref, c_ref, sa_ref, sb_ref, mu_ref, prev_ref,
                   q_ref, kf_ref, kb_ref, vf_ref, vb_ref, qi_ref, kiwi_ref, kib_ref, sh_ref, last_ref,
                   gate_ref, carry_ref):
    tm = x_ref.shape[0]
    xn = _rms(x_ref[...], g_ref[...]).astype(BF16)
    c, sa, sb = c_ref[...], sa_ref[...], sb_ref[...]

    def proj(c0, c1):
        return _dot(xn, w_ref[:, c0:c1])

    def rope(z):
        return jnp.concatenate(
            [_rope_slab(z[:, s:s + LANES], c, sa, sb) for s in range(0, z.shape[1], LANES)], axis=1)

    q = rope(proj(0, A_WIDTH)) * (HEAD_DIM ** -0.5)
    q_ref[...] = q.astype(BF16)
    k = rope(proj(A_WIDTH, 2 * A_WIDTH))
    kf_ref[...] = k
    kb_ref[...] = k.astype(BF16)
    v = proj(2 * A_WIDTH, QKV_END)
    vf_ref[...] = v
    vb_ref[...] = v.astype(BF16)
    qi_ref[...] = rope(proj(QKV_END, QI_END)).astype(BF16)
    kiwi = proj(QI_END, KIWI_END)
    lane = lax.broadcasted_iota(jnp.int32, kiwi.shape, 1)
    kiwi = jnp.where(lane < IDX_DIM, _rope_slab(kiwi, c, sa, sb), kiwi * IDX_W_SCALE)
    kiwi_ref[...] = kiwi
    kib_ref[...] = kiwi[:, :IDX_DIM].astype(BF16)

    u = proj(KIWI_END, SHIFT_END)
    if seq_mode:
        @pl.when(pl.program_id(0) == 0)
        def _():
            carry_ref[...] = prev_ref[...]
        row = lax.broadcasted_iota(jnp.int32, u.shape, 0)
        u_prev = jnp.where(row == 0, carry_ref[...], pltpu.roll(u, 1, 0))
        carry_ref[...] = u[tm - 1:tm, :]
    else:
        u_prev = prev_ref[...]
    sh_ref[...] = u + (u_prev - u) * mu_ref[...]
    last_ref[...] = u[tm - 1:tm, :] if seq_mode else u
    gate_ref[...] = jax.nn.sigmoid(proj(SHIFT_END, D_IN_PACKED))


def _inproj(x, norm_g, w_packed, rope_c, rope_sa, rope_sb, mu, prev, seq_mode, tm):
    t = x.shape[0]
    assert t % tm == 0
    row = lambda w: pl.BlockSpec((tm, w), lambda i: (i, 0))
    const = lambda a: pl.BlockSpec(a.shape, lambda i: (0, 0))
    prev_spec = const(prev) if seq_mode else row(SHIFT_W)
    out_shape = (
        jax.ShapeDtypeStruct((t, A_WIDTH), BF16),
        jax.ShapeDtypeStruct((t, A_WIDTH), F32), jax.ShapeDtypeStruct((t, A_WIDTH), BF16),
        jax.ShapeDtypeStruct((t, A_WIDTH), F32), jax.ShapeDtypeStruct((t, A_WIDTH), BF16),
        jax.ShapeDtypeStruct((t, IDX_HEADS * IDX_DIM), BF16),
        jax.ShapeDtypeStruct((t, LANES), F32),
        jax.ShapeDtypeStruct((t, IDX_DIM), BF16),
        jax.ShapeDtypeStruct((t, SHIFT_W), F32),
        jax.ShapeDtypeStruct((1 if seq_mode else t, SHIFT_W), F32),
        jax.ShapeDtypeStruct((t, 2 * D_MODEL), F32),
    )
    out_specs = (row(A_WIDTH), row(A_WIDTH), row(A_WIDTH), row(A_WIDTH), row(A_WIDTH),
                 row(IDX_HEADS * IDX_DIM), row(LANES), row(IDX_DIM), row(SHIFT_W),
                 pl.BlockSpec((1, SHIFT_W), lambda i: (0, 0)) if seq_mode else row(SHIFT_W), row(2 * D_MODEL))
    return pl.pallas_call(
        functools.partial(_inproj_kernel, seq_mode),
        grid=(t // tm,),
        in_specs=[row(D_MODEL), const(norm_g), const(w_packed), row(LANES), row(LANES), row(LANES),
                  const(mu), prev_spec],
        out_specs=out_specs, out_shape=out_shape,
        scratch_shapes=[pltpu.VMEM((1, SHIFT_W), F32)],
        compiler_params=_cparams(("arbitrary",)),
        name="inproj_seq" if seq_mode else "inproj_batch",
    )(x, norm_g, w_packed, rope_c, rope_sa, rope_sb, mu, prev)


def _rope_tables(pos):
    half = ROT_DIM // 2
    inv = ROPE_THETA ** (-jnp.arange(half, dtype=F32) / half)
    ang = pos.astype(F32)[:, None] * inv[None, :]
    cos, sin = jnp.cos(ang), jnp.sin(ang)
    t = pos.shape[0]
    pad = jnp.zeros((t, HEAD_DIM - ROT_DIM), F32)
    zero = jnp.zeros((t, half), F32)
    c = jnp.concatenate([cos, cos, pad + 1.0], 1)
    sa = jnp.concatenate([zero, sin, pad], 1)
    sb = jnp.concatenate([-sin, zero, pad], 1)
    rep = lambda a: jnp.concatenate([a] * (LANES // HEAD_DIM), 1)
    return rep(c), rep(sa), rep(sb)


def _pack_w_in(w_in):
    lead = QI_END + IDX_DIM + IDX_HEADS
    pad = jnp.zeros((D_MODEL, KIWI_END - lead), w_in.dtype)
    return jnp.concatenate([w_in[:, :lead], pad, w_in[:, lead:]], 1).astype(BF16)


DSA_TQ = 256
DSA_TK = 512
THR_ROWS = 128
THR_WARM_STEPS = 10
THR_CHEAP_STEPS = 1
THR_UNROLL = DSA_TK // LANES


def _causal_pairs(t, tq, tk):
    qi, kj, last = [], [], []
    for i in range(t // tq):
        nk = (i * tq + tq - 1) // tk + 1
        for j in range(nk):
            qi.append(i), kj.append(j), last.append(int(j == nk - 1))
    return (jnp.asarray(qi, jnp.int32), jnp.asarray(kj, jnp.int32), jnp.asarray(last, jnp.int32))


def _lane_tile(x, width):
    return jnp.concatenate([x] * (width // LANES), axis=1)


def _index_scores(qi, ki, wi, q0, k0):
    tq, tk = qi.shape[0], ki.shape[0]
    s = None
    for h in range(IDX_HEADS):
        sh = jnp.maximum(_dot_t(qi[:, h * IDX_DIM:(h + 1) * IDX_DIM], ki), 0.0)
        sh = sh * _lane_tile(wi[:, h * LANES:(h + 1) * LANES], tk)
        s = sh if s is None else s + sh
    qpos = q0 + lax.broadcasted_iota(jnp.int32, (tq, tk), 0)
    kpos = k0 + lax.broadcasted_iota(jnp.int32, (tq, tk), 1)
    return jnp.where(kpos <= qpos, s, -jnp.inf)


def _kth_largest_rows(sc_ref, r0, ncol, kf):
    rows = pl.ds(r0, THR_ROWS)
    shape = (THR_ROWS, LANES)

    def col_reduce(fn, init, combine):
        def body(c, acc):
            for u in range(THR_UNROLL):
                blk = sc_ref[rows, pl.ds(pl.multiple_of((c * THR_UNROLL + u) * LANES, LANES), LANES)]
                acc = combine(acc, fn(blk))
            return acc
        return lax.fori_loop(0, ncol // THR_UNROLL, body, jnp.full(shape, init, F32))

    def count_gt(m):
        mb = jnp.broadcast_to(m, shape)
        acc = col_reduce(lambda b: jnp.where(b > mb, 1.0, 0.0), 0.0, jnp.add)
        return jnp.sum(acc, axis=-1, keepdims=True)

    def snap(lx, hi):
        lb, hb = jnp.broadcast_to(lx, shape), jnp.broadcast_to(hi, shape)
        lo_acc = col_reduce(lambda b: jnp.where(b > lb, b, jnp.inf), jnp.inf, jnp.minimum)
        hi_acc = col_reduce(lambda b: jnp.where(b <= hb, b, -jnp.inf), -jnp.inf, jnp.maximum)
        return jnp.min(lo_acc, axis=-1, keepdims=True), jnp.max(hi_acc, axis=-1, keepdims=True)

    def bisect(_, bounds):
        lx, hi, a, b = bounds
        m = 0.5 * a + 0.5 * b
        up = count_gt(m) >= kf
        return jnp.where(up, m, lx), jnp.where(up, hi, m), jnp.where(up, m, a), jnp.where(up, b, m)

    def round_(state):
        lx, hi, _, _ = state
        d_lo, d_hi = snap(lx, hi)
        m = 0.5 * d_lo + 0.5 * d_hi
        m = jnp.where(m < d_hi, m, d_lo)
        up = count_gt(m) >= kf
        lx, hi = jnp.where(up, m, lx), jnp.where(up, hi, m)
        bounds = (lx, hi, jnp.maximum(lx, d_lo), jnp.minimum(hi, d_hi))
        for step in range(THR_CHEAP_STEPS):
            bounds = bisect(step, bounds)
        open_rows = jnp.max(jnp.where(d_lo < d_hi, 1.0, 0.0))
        return bounds[0], bounds[1], d_lo, open_rows

    lx = jnp.full((THR_ROWS, 1), -jnp.inf, F32)
    hi = jnp.full((THR_ROWS, 1), jnp.inf, F32)
    d_lo, d_hi = snap(lx, hi)
    lx, hi, _, _ = lax.fori_loop(0, THR_WARM_STEPS, bisect, (lx, hi, d_lo, d_hi))
    init = (lx, hi, jnp.zeros((THR_ROWS, 1), F32), jnp.float32(1.0))
    lx, hi, thr, _ = lax.while_loop(lambda s: s[3] > 0.0, round_, init)
    tb = jnp.broadcast_to(thr, shape)
    n_gt = jnp.sum(col_reduce(lambda b: jnp.where(b > tb, 1.0, 0.0), 0.0, jnp.add), axis=-1, keepdims=True)
    n_ge = jnp.sum(col_reduce(lambda b: jnp.where(b >= tb, 1.0, 0.0), 0.0, jnp.add), axis=-1, keepdims=True)
    return thr, n_gt, n_ge


def _tie_cutoff_rows(sc_ref, r0, ncol, thr, need):
    rows = pl.ds(r0, THR_ROWS)
    shape = (THR_ROWS, LANES)
    tb = jnp.broadcast_to(thr, shape)
    lane = lax.broadcasted_iota(jnp.int32, shape, 1)

    def count_le(j):
        jb = jnp.broadcast_to(j, shape)

        def body(c, acc):
            blk = sc_ref[rows, pl.ds(pl.multiple_of(c * LANES, LANES), LANES)]
            hit = jnp.logical_and(blk == tb, lane + c * LANES <= jb)
            return acc + jnp.where(hit, 1.0, 0.0)
        return jnp.sum(lax.fori_loop(0, ncol, body, jnp.zeros(shape, F32)), axis=-1, keepdims=True)

    def step(_, lh):
        lo, hi = lh
        mid = (lo + hi) // 2
        ok = count_le(mid) >= need
        return jnp.where(ok, lo, mid), jnp.where(ok, mid, hi)

    lo = jnp.full((THR_ROWS, 1), -1, jnp.int32)
    hi = jnp.full((THR_ROWS, 1), 1, jnp.int32) * (ncol * LANES - 1)
    n_steps = int(np.ceil(np.log2(sc_ref.shape[1] + 1))) + 1
    _, hi = lax.fori_loop(0, n_steps, step, (lo, hi))
    return hi


def _dsa_thr_kernel(qi_blk, kj_blk, last_blk, qi_ref, wi_ref, ki_ref, thr_ref, cut_ref, sc_ref):
    p = pl.program_id(0)
    i, j = qi_blk[p], kj_blk[p]
    tq, tk = qi_ref.shape[0], ki_ref.shape[0]
    s = _index_scores(qi_ref[...], ki_ref[...], wi_ref[...], i * tq, j * tk)
    sc_ref[:, pl.ds(pl.multiple_of(j * tk, tk), tk)] = s

    @pl.when(last_blk[p] == 1)
    def _():
        ncol = (j + 1) * (tk // LANES)
        for g in range(tq // THR_ROWS):
            r0 = g * THR_ROWS
            qpos = i * tq + r0 + lax.broadcasted_iota(jnp.int32, (THR_ROWS, 1), 0)
            kf = jnp.minimum(qpos + 1, TOPK_MAX).astype(F32)
            thr, n_gt, n_ge = _kth_largest_rows(sc_ref, r0, ncol, kf)
            thr_ref[r0:r0 + THR_ROWS, :] = jnp.broadcast_to(thr, (THR_ROWS, LANES))
            cut_ref[r0:r0 + THR_ROWS, :] = jnp.full((THR_ROWS, LANES), sc_ref.shape[1], jnp.int32)

            @pl.when(jnp.max(n_ge - kf) > 0.0)
            def _():
                cut = _tie_cutoff_rows(sc_ref, r0, ncol, thr, kf - n_gt)
                cut = jnp.where(n_ge > kf, cut, sc_ref.shape[1])
                cut_ref[r0:r0 + THR_ROWS, :] = jnp.broadcast_to(cut, (THR_ROWS, LANES))


ATTN_ROWS = 64


def _dsa_attn_kernel(qi_blk, kj_blk, last_blk, q_ref, qi_ref, wi_ref, thr_ref, cut_ref, ki_ref, k_ref, vx_ref,
                     o_ref, m_ref, acc_ref, bias_ref, s_ref, p_ref, alpha_ref):
    p = pl.program_id(0)
    i, j = qi_blk[p], kj_blk[p]
    tq, tk = q_ref.shape[1], k_ref.shape[1]

    @pl.when(j == 0)
    def _():
        m_ref[...] = jnp.full(m_ref.shape, NEG_BIG, F32)
        acc_ref[...] = jnp.zeros(acc_ref.shape, F32)

    s = _index_scores(qi_ref[...], ki_ref[...], wi_ref[...], i * tq, j * tk)
    thr = _lane_tile(thr_ref[...], tk)
    kpos = j * tk + lax.broadcasted_iota(jnp.int32, (tq, tk), 1)
    sel = jnp.logical_or(s > thr, jnp.logical_and(s == thr, kpos <= _lane_tile(cut_ref[...], tk)))
    bias_ref[...] = jnp.where(sel, 0.0, NEG_BIG)

    def logits(h):
        s_ref[h % 2] = _dot_t(q_ref[h], k_ref[h])

    def softmax(h):
        slot = h % 2
        for r0 in range(0, tq, ATTN_ROWS):
            rows = slice(r0, r0 + ATTN_ROWS)
            x = s_ref[slot, rows, :] + bias_ref[rows, :]
            m_prev = m_ref[h, rows, :]
            m_new = jnp.maximum(m_prev, jnp.max(x, axis=-1, keepdims=True))
            p_ref[slot, rows, :] = jnp.exp(x - _lane_tile(m_new, tk)).astype(BF16)
            alpha_ref[slot, rows, :] = jnp.exp(m_prev - m_new)
            m_ref[h, rows, :] = m_new

    def values(h):
        acc_ref[h] = alpha_ref[h % 2] * acc_ref[h] + _dot(p_ref[h % 2], vx_ref[h])

    logits(0)
    logits(1)
    softmax(0)

    def head_step(h, carry):
        values(h - 2)
        softmax(h - 1)
        logits(h)
        return carry
    lax.fori_loop(2, A_HEADS, head_step, 0)
    softmax(A_HEADS - 1)
    values(A_HEADS - 2)
    values(A_HEADS - 1)

    @pl.when(last_blk[p] == 1)
    def _():
        o_ref[...] = jnp.concatenate(
            [acc_ref[h][:, 0:HEAD_DIM] / acc_ref[h][:, HEAD_DIM:HEAD_DIM + 1] for h in range(A_HEADS)], axis=1)


def _dsa_prompt(q, k, v, qi, kiwi, ki):
    t = q.shape[0]
    tq, tk = min(DSA_TQ, t), min(DSA_TK, t)
    assert t % tq == 0 and t % tk == 0 and tq % THR_ROWS == 0
    kiwi = jnp.repeat(kiwi[:, IDX_DIM:IDX_DIM + IDX_HEADS], LANES, axis=1)
    pairs = _causal_pairs(t, tq, tk)
    n_pairs = pairs[0].shape[0]
    qrow = lambda w: pl.BlockSpec((tq, w), lambda p, qb, kb, lb: (qb[p], 0))
    krow = lambda w: pl.BlockSpec((tk, w), lambda p, qb, kb, lb: (kb[p], 0))
    thr, cut = pl.pallas_call(
        _dsa_thr_kernel,
        grid_spec=pltpu.PrefetchScalarGridSpec(
            num_scalar_prefetch=3, grid=(n_pairs,),
            in_specs=[qrow(IDX_HEADS * IDX_DIM), qrow(IDX_HEADS * LANES), krow(IDX_DIM)],
            out_specs=(qrow(LANES), qrow(LANES)),
            scratch_shapes=[pltpu.VMEM((tq, t), F32)]),
        out_shape=(jax.ShapeDtypeStruct((t, LANES), F32), jax.ShapeDtypeStruct((t, LANES), jnp.int32)),
        compiler_params=_cparams(("arbitrary",)), name="dsa_prompt_threshold",
    )(*pairs, qi, kiwi, ki)
    qhead = pl.BlockSpec((A_HEADS, tq, HEAD_DIM), lambda p, qb, kb, lb: (0, qb[p], 0))
    khead = lambda w: pl.BlockSpec((A_HEADS, tk, w), lambda p, qb, kb, lb: (0, kb[p], 0))
    heads = lambda a: jnp.swapaxes(a.reshape(t, A_HEADS, HEAD_DIM), 0, 1)
    return pl.pallas_call(
        _dsa_attn_kernel,
        grid_spec=pltpu.PrefetchScalarGridSpec(
            num_scalar_prefetch=3, grid=(n_pairs,),
            in_specs=[qhead, qrow(IDX_HEADS * IDX_DIM), qrow(IDX_HEADS * LANES), qrow(LANES), qrow(LANES),
                      krow(IDX_DIM), khead(HEAD_DIM), khead(LANES)],
            out_specs=qrow(A_WIDTH),
            scratch_shapes=[pltpu.VMEM((A_HEADS, tq, LANES), F32), pltpu.VMEM((A_HEADS, tq, LANES), F32),
                            pltpu.VMEM((tq, tk), F32), pltpu.VMEM((2, tq, tk), F32),
                            pltpu.VMEM((2, tq, tk), BF16), pltpu.VMEM((2, tq, LANES), F32)]),
        out_shape=jax.ShapeDtypeStruct((t, A_WIDTH), F32),
        compiler_params=_cparams(("arbitrary",)), name="dsa_prompt_attention",
    )(*pairs, heads(q), qi, kiwi, thr, cut, ki, heads(k), _values_with_ones(heads(v)))


def _values_with_ones(v3):
    ones = jnp.ones(v3.shape[:2] + (1,), v3.dtype)
    pad = jnp.zeros(v3.shape[:2] + (LANES - HEAD_DIM - 1,), v3.dtype)
    return jnp.concatenate([v3, ones, pad], axis=-1)


IDX_ROWS = 16


def _dsa_sample_scores_kernel(pt_ref, qi_ref, wi_ref, kin_ref, cache_ref, o_ref, buf_ref, sem_ref):
    b, nb = pl.program_id(0), pl.num_programs(0)
    n_pages = pt_ref.shape[1]
    past = n_pages * PAGE_SIZE

    def issue(bb, slot):
        for p in range(n_pages):
            pltpu.make_async_copy(cache_ref.at[0, pt_ref[bb, p]], buf_ref.at[slot, p], sem_ref.at[slot]).start()

    @pl.when(b == 0)
    def _():
        issue(0, 0)

    @pl.when(b + 1 < nb)
    def _():
        issue(b + 1, (b + 1) % 2)

    slot = b % 2
    pltpu.make_async_copy(buf_ref.at[slot], buf_ref.at[slot], sem_ref.at[slot]).wait()
    qi, wi = qi_ref[0], wi_ref[0]

    def page(p, carry):
        s = jnp.maximum(_dot(qi, buf_ref[slot, p].astype(BF16)), 0.0) * wi
        o_ref[0, pl.ds(p, 1), :] = jnp.sum(s, axis=0, keepdims=True)
        return carry
    lax.fori_loop(0, n_pages, page, 0)
    kin = kin_ref[0].astype(F32)
    s_new = jnp.maximum(jnp.sum(qi.astype(F32) * kin, axis=-1, keepdims=True), 0.0) * wi
    s_new = jnp.sum(s_new, axis=0, keepdims=True)
    lane = lax.broadcasted_iota(jnp.int32, (1, LANES), 1)
    o_ref[0, n_pages:n_pages + 1, :] = jnp.where(lane == 0, s_new, -jnp.inf)


def _dsa_sample_scores(page_table, qi3, wi3, ki_new, cache_idx_t):
    bsz, n_pages = page_table.shape
    blk = lambda a: pl.BlockSpec((1,) + a.shape[1:], lambda b, pt: (b, 0, 0))
    return pl.pallas_call(
        _dsa_sample_scores_kernel,
        grid_spec=pltpu.PrefetchScalarGridSpec(
            num_scalar_prefetch=1, grid=(bsz,),
            in_specs=[blk(qi3), blk(wi3), blk(ki_new), pl.BlockSpec(memory_space=pl.ANY)],
            out_specs=pl.BlockSpec((1, n_pages + 1, LANES), lambda b, pt: (b, 0, 0)),
            scratch_shapes=[pltpu.VMEM((2, n_pages, IDX_DIM, PAGE_SIZE), F32), pltpu.SemaphoreType.DMA((2,))]),
        out_shape=jax.ShapeDtypeStruct((bsz, n_pages + 1, LANES), F32),
        compiler_params=_cparams(("arbitrary",)), name="dsa_sample_scores",
    )(page_table, qi3, wi3, ki_new, cache_idx_t)


def _topk_mask_kernel(k_sel, sc_ref, mask_ref, work_ref):
    bsz, width = sc_ref.shape
    ncol = width // LANES
    shape = (bsz, LANES)
    work_ref[...] = sc_ref[...]
    lane = lax.broadcasted_iota(jnp.int32, shape, 1)

    def chunk(c):
        return work_ref[:, pl.ds(pl.multiple_of(c * LANES, LANES), LANES)]

    def body(r, carry):
        mx = lax.fori_loop(0, ncol, lambda c, a: jnp.maximum(a, chunk(c)), jnp.full(shape, -jnp.inf, F32))
        mxb = jnp.broadcast_to(jnp.max(mx, axis=-1, keepdims=True), shape)
        big = jnp.int32(width)
        ix = lax.fori_loop(0, ncol, lambda c, a: jnp.minimum(a, jnp.where(chunk(c) == mxb, lane + c * LANES, big)),
                           jnp.full(shape, width, jnp.int32))
        ixb = jnp.broadcast_to(jnp.min(ix, axis=-1, keepdims=True), shape)

        def knock(c, inner):
            work_ref[:, pl.ds(pl.multiple_of(c * LANES, LANES), LANES)] = jnp.where(
                lane + c * LANES == ixb, -jnp.inf, chunk(c))
            return inner
        lax.fori_loop(0, ncol, knock, 0)
        return carry

    lax.fori_loop(0, k_sel + 0 * pl.program_id(0), body, 0)
    mask_ref[...] = jnp.where(work_ref[...] != sc_ref[...], 1.0, 0.0)


def _topk_mask(scores, k_sel):
    bsz, width = scores.shape
    whole = pl.BlockSpec((bsz, width), lambda i: (0, 0))
    return pl.pallas_call(
        functools.partial(_topk_mask_kernel, k_sel), grid=(1,),
        in_specs=[whole], out_specs=whole,
        out_shape=jax.ShapeDtypeStruct((bsz, width), F32),
        scratch_shapes=[pltpu.VMEM((bsz, width), F32)],
        compiler_params=_cparams(("arbitrary",)), name="dsa_sample_topk",
    )(scores)


SAMPLE_PAGES_PER_STEP = 8


def _dsa_sample_attn_kernel(pt_ref, qt_ref, knt_ref, vnt_ref, mask_ref, ck_ref, cv_ref, o_ref,
                            kbuf, vbuf, sem_ref, m_ref, l_ref, acc_ref):
    b, c = pl.program_id(0), pl.program_id(1)
    nb, nc = pl.num_programs(0), pl.num_programs(1)
    pps = SAMPLE_PAGES_PER_STEP
    n_pages = pt_ref.shape[1]
    step = b * nc + c

    def issue(s, slot):
        bb, cc = s // nc, s % nc
        for i in range(pps):
            page = pt_ref[bb, cc * pps + i]
            pltpu.make_async_copy(ck_ref.at[0, page], kbuf.at[slot, i], sem_ref.at[0, slot]).start()
            pltpu.make_async_copy(cv_ref.at[0, page], vbuf.at[slot, i], sem_ref.at[1, slot]).start()

    @pl.when(step == 0)
    def _():
        issue(0, 0)

    @pl.when(step + 1 < nb * nc)
    def _():
        issue(step + 1, (step + 1) % 2)

    slot = step % 2
    pltpu.make_async_copy(kbuf.at[slot], kbuf.at[slot], sem_ref.at[0, slot]).wait()
    pltpu.make_async_copy(vbuf.at[slot], vbuf.at[slot], sem_ref.at[1, slot]).wait()

    @pl.when(c == 0)
    def _():
        m_ref[...] = jnp.full(m_ref.shape, NEG_BIG, F32)
        l_ref[...] = jnp.zeros(l_ref.shape, F32)
        acc_ref[...] = jnp.zeros(acc_ref.shape, F32)

    keep = mask_ref[0, pl.ds(pl.multiple_of(c * pps, pps), pps), :] > 0.0
    qt = qt_ref[0].astype(F32)
    for h in range(A_HEADS):
        qcol = qt[:, h:h + 1]
        lg = jnp.concatenate([jnp.sum(kbuf[slot, i, h] * qcol, axis=0, keepdims=True) for i in range(pps)], axis=0)
        lg = jnp.where(keep, lg, NEG_BIG)
        m_old = m_ref[h]
        m_new = jnp.maximum(m_old, jnp.max(lg, axis=(0, 1), keepdims=True))
        alpha = jnp.exp(m_old - m_new)
        p = jnp.where(keep, jnp.exp(lg - m_new[0:1, :]), 0.0)
        l_ref[h] = alpha * l_ref[h] + jnp.sum(p, axis=(0, 1), keepdims=True)
        pv = vbuf[slot, 0, h] * p[0:1, :]
        for i in range(1, pps):
            pv = pv + vbuf[slot, i, h] * p[i:i + 1, :]
        acc_ref[h] = alpha[0:1, :] * acc_ref[h] + pv
        m_ref[h] = m_new

    @pl.when(c == nc - 1)
    def _():
        lane = lax.broadcasted_iota(jnp.int32, (HEAD_DIM, LANES), 1)
        keep_new = mask_ref[0, n_pages:n_pages + 1, 0:1] > 0.0
        out = jnp.zeros((HEAD_DIM, LANES), F32)
        for h in range(A_HEADS):
            lg_new = jnp.sum(knt_ref[0][:, h:h + 1] * qt[:, h:h + 1], axis=0, keepdims=True)
            lg_new = jnp.where(keep_new, lg_new, NEG_BIG)
            m_old = m_ref[h][0:1, 0:1]
            m_new = jnp.maximum(m_old, lg_new)
            alpha = jnp.exp(m_old - m_new)
            p_new = jnp.where(keep_new, jnp.exp(lg_new - m_new), 0.0)
            l_fin = alpha * l_ref[h][0:1, 0:1] + p_new
            col = alpha * jnp.sum(acc_ref[h], axis=-1, keepdims=True) + p_new * vnt_ref[0][:, h:h + 1]
            out = jnp.where(lane == h, col / l_fin, out)
        o_ref[0] = out


def _dsa_sample_attn(page_table, q_t, k_new_t, v_new_t, mask3, cache_k_t, cache_v_t):
    bsz, n_pages = page_table.shape
    pps = SAMPLE_PAGES_PER_STEP
    assert n_pages % pps == 0
    col = lambda: pl.BlockSpec((1, HEAD_DIM, A_HEADS), lambda b, c, pt: (b, 0, 0))
    page_buf = pltpu.VMEM((2, pps, A_HEADS, HEAD_DIM, PAGE_SIZE), F32)
    return pl.pallas_call(
        _dsa_sample_attn_kernel,
        grid_spec=pltpu.PrefetchScalarGridSpec(
            num_scalar_prefetch=1, grid=(bsz, n_pages // pps),
            in_specs=[col(), col(), col(), pl.BlockSpec((1, n_pages + 1, LANES), lambda b, c, pt: (b, 0, 0)),
                      pl.BlockSpec(memory_space=pl.ANY), pl.BlockSpec(memory_space=pl.ANY)],
            out_specs=pl.BlockSpec((1, HEAD_DIM, LANES), lambda b, c, pt: (b, 0, 0)),
            scratch_shapes=[page_buf, page_buf, pltpu.SemaphoreType.DMA((2, 2)),
                            pltpu.VMEM((A_HEADS, SUBLANES, LANES), F32), pltpu.VMEM((A_HEADS, SUBLANES, LANES), F32),
                            pltpu.VMEM((A_HEADS, HEAD_DIM, LANES), F32)]),
        out_shape=jax.ShapeDtypeStruct((bsz, HEAD_DIM, LANES), F32),
        compiler_params=_cparams(("arbitrary", "arbitrary")), name="dsa_sample_attention",
    )(page_table, q_t, k_new_t, v_new_t, mask3, cache_k_t, cache_v_t)


def _head_ones():
    h = np.arange(B_WIDTH) // B_HEAD_DIM
    return jnp.asarray((h[:, None] == h[None, :]).astype(np.float32))


def _rwkv_pre_kernel(sh_ref, w0_ref, wup_ref, a0_ref, aup_ref, gup_ref, kk_ref_, ka_ref, g1_ref,
                     ld_ref, kk_ref, kb_ref, kp_ref, g_ref):
    k = sh_ref[:, B_WIDTH:2 * B_WIDTH]
    o = 3 * B_WIDTH
    xw = sh_ref[:, o:o + DECAY_LORA]
    xa = sh_ref[:, o + DECAY_LORA:o + DECAY_LORA + AAA_LORA]
    xg = sh_ref[:, o + DECAY_LORA + AAA_LORA:SHIFT_W]
    w = w0_ref[...] + _dotx(jnp.tanh(xw), wup_ref[...])
    logw = -jax.nn.softplus(-w) - 0.5
    ld_ref[...] = -jnp.exp(logw)
    a = jax.nn.sigmoid(a0_ref[...] + _dotx(xa, aup_ref[...]))
    g_ref[...] = _dotx(jax.nn.sigmoid(xg), gup_ref[...])
    kk = k * kk_ref_[...]
    kk = kk / jnp.maximum(jnp.sqrt(_dotx(kk * kk, g1_ref[...])), 1e-12)
    kk_ref[...] = kk
    kb_ref[...] = kk * a
    kp_ref[...] = k * (1.0 + (a - 1.0) * ka_ref[...])


def _rwkv_pre(shifted, w0, w_lora_up, a0, a_lora_up, g_lora_up, k_k, k_a, tm):
    t = shifted.shape[0]
    row = lambda w: pl.BlockSpec((tm, w), lambda i: (i, 0))
    const = lambda a: pl.BlockSpec(a.shape, lambda i: (0, 0))
    g1 = _head_ones()
    args = (shifted, w0, w_lora_up, a0, a_lora_up, g_lora_up, k_k, k_a, g1)
    out = jax.ShapeDtypeStruct((t, B_WIDTH), F32)
    return pl.pallas_call(
        _rwkv_pre_kernel, grid=(t // tm,),
        in_specs=[row(SHIFT_W)] + [const(a) for a in args[1:]],
        out_specs=(row(B_WIDTH),) * 5, out_shape=(out,) * 5,
        compiler_params=_cparams(("arbitrary",)), name="rwkv_prep",
    )(*args)


def _rwkv_seq_kernel(r_ref, v_ref, ld_ref, kk_ref, kb_ref, kp_ref, y_ref, so_ref, st_ref):
    c = CHUNK

    @pl.when(pl.program_id(0) == 0)
    def _():
        st_ref[...] = jnp.zeros(st_ref.shape, F32)

    ri = lax.broadcasted_iota(jnp.int32, (c, c), 0)
    ci = lax.broadcasted_iota(jnp.int32, (c, c), 1)
    incl, strict, eye = ri >= ci, ri > ci, ri == ci
    ld = ld_ref[...]
    tri = jnp.where(incl, 1.0, 0.0).astype(BF16)
    ld_hi = ld.astype(BF16)
    rest = ld - ld_hi.astype(F32)
    ld_mid = rest.astype(BF16)
    ld_lo = (rest - ld_mid.astype(F32)).astype(BF16)
    cum = _dot(tri, ld_hi) + (_dot(tri, ld_mid) + _dot(tri, ld_lo))
    cum_c = cum[c - 1:c, :]
    e_neg, e_rem = jnp.exp(-cum), jnp.exp(cum_c - cum)
    kk, kb, kp = kk_ref[...], kb_ref[...], kp_ref[...]
    a_all = -kk * jnp.exp(cum - ld)
    r_all = r_ref[...] * jnp.exp(cum)
    bt_all, kt_all = kb * e_neg, kp * e_neg
    bh_all, kh_all = kb * e_rem, kp * e_rem
    pc = jnp.exp(cum_c)
    v_all = v_ref[...]
    n = B_HEAD_DIM
    heads = range(B_HEADS)
    hsl = [slice(h * n, (h + 1) * n) for h in heads]
    p = [_dot3_t(jnp.concatenate([a_all[:, s], r_all[:, s]], 0), jnp.concatenate([bt_all[:, s], kt_all[:, s]], 0))
         for s in hsl]
    l_ak = [jnp.where(strict, q[:c, c:], 0.0) for q in p]
    m_rb = [jnp.where(incl, q[c:, :c], 0.0) for q in p]
    m_rk = [jnp.where(incl, q[c:, c:], 0.0) for q in p]
    lp = [jnp.where(strict, q[:c, :c], 0.0) for q in p]
    x = [jnp.concatenate([a_all[:, s], _dot3(l_ak[h], v_all[:, s])], 1) for h, s in enumerate(hsl)]
    n_dbl = int(np.log2(c))
    for it in range(n_dbl):
        x = [x[h] + _dot3(lp[h], x[h]) for h in heads]
        if it + 1 < n_dbl:
            lp = [_dot3(lp[h], lp[h]) for h in heads]
    bm = [_dot3(jnp.concatenate([bh_all[:, s].T, m_rb[h]], 0), x[h]) for h, s in enumerate(hsl)]
    kv = [_dot3(jnp.concatenate([kh_all[:, s].T, m_rk[h]], 0), v_all[:, s]) for h, s in enumerate(hsl)]
    for h, s in enumerate(hsl):
        m_c = jnp.where(eye, jnp.broadcast_to(pc[:, s], (n, n)), 0.0) + bm[h][:n, :n]
        g_c = r_all[:, s] + bm[h][n:, :n]
        gs = _dot3(jnp.concatenate([g_c, m_c], 0), st_ref[h])
        y_ref[:, s] = gs[:c, :] + (bm[h][n:, n:] + kv[h][n:, :])
        st_ref[h] = gs[c:, :] + (bm[h][:n, n:] + kv[h][:n, :])
    so_ref[...] = st_ref[...]


def _rwkv_seq(shifted, ld, kk, kb, kp):
    t = shifted.shape[0]
    assert t % CHUNK == 0
    col = lambda j: pl.BlockSpec((CHUNK, B_WIDTH), lambda i: (i, j))
    state = jax.ShapeDtypeStruct((B_HEADS, B_HEAD_DIM, B_HEAD_DIM), F32)
    return pl.pallas_call(
        _rwkv_seq_kernel, grid=(t // CHUNK,),
        in_specs=[col(0), col(2), col(0), col(0), col(0), col(0)],
        out_specs=(col(0), pl.BlockSpec(state.shape, lambda i: (0, 0, 0))),
        out_shape=(jax.ShapeDtypeStruct((t, B_WIDTH), F32), state),
        scratch_shapes=[pltpu.VMEM(state.shape, F32)],
        compiler_params=_cparams(("arbitrary",)), name="rwkv_chunked_scan",
    )(shifted, shifted, ld, kk, kb, kp)


RWKV_STEP_ROWS = 8


def _rwkv_step_kernel(s_ref, r_ref, ld_ref, kk_ref, kb_ref, kp_ref, v_ref, so_ref, y_ref):
    def one(b, carry):
        row = pl.ds(b, 1)
        r, w, kk, kb, kp = r_ref[row, :], jnp.exp(ld_ref[row, :]), kk_ref[row, :], kb_ref[row, :], kp_ref[row, :]
        for h in range(B_HEADS):
            hs = slice(h * B_HEAD_DIM, (h + 1) * B_HEAD_DIM)
            s = s_ref[b, h]
            sa = -jnp.sum(s * kk[:, hs], axis=-1, keepdims=True)
            s2 = s * w[:, hs] + sa * kb[:, hs] + v_ref[b, h] * kp[:, hs]
            so_ref[b, h] = s2
            y_ref[b, h] = jnp.sum(s2 * r[:, hs], axis=-1, keepdims=True)
        return carry
    lax.fori_loop(0, s_ref.shape[0], one, 0)


def _rwkv_step(state, shifted, ld, kk, kb, kp, v_col):
    bsz = state.shape[0]
    nb = RWKV_STEP_ROWS
    assert bsz % nb == 0
    row = lambda: pl.BlockSpec((nb, B_WIDTH), lambda i: (i, 0))
    st = pl.BlockSpec((nb, B_HEADS, B_HEAD_DIM, B_HEAD_DIM), lambda i: (i, 0, 0, 0))
    colv = pl.BlockSpec((nb, B_HEADS, B_HEAD_DIM, 1), lambda i: (i, 0, 0, 0))
    return pl.pallas_call(
        _rwkv_step_kernel, grid=(bsz // nb,),
        in_specs=[st, row(), row(), row(), row(), row(), colv],
        out_specs=(st, colv),
        out_shape=(jax.ShapeDtypeStruct(state.shape, F32),
                   jax.ShapeDtypeStruct((bsz, B_HEADS, B_HEAD_DIM, 1), F32)),
        compiler_params=_cparams(("arbitrary",)), name="rwkv_single_step",
    )(state, shifted, ld, kk, kb, kp, v_col)


def _rwkv_post_kernel(y_ref, r_ref, v_ref, kp_ref, g_ref, rk_ref, gng_ref, gnb_ref, g1_ref, o_ref):
    g1 = g1_ref[...]
    inv_n = 1.0 / B_HEAD_DIM
    y = y_ref[...]
    d = y - _dotx(y, g1) * inv_n
    var = _dotx(d * d, g1) * inv_n
    yn = d * lax.rsqrt(var + GN_EPS) * gng_ref[...] + gnb_ref[...]
    bonus = _dotx(r_ref[...] * kp_ref[...] * rk_ref[...], g1) * v_ref[...]
    o_ref[...] = (yn + bonus) * g_ref[...]


def _rwkv_post(y, shifted, kp, g, r_k, gn_g, gn_b, tm):
    t = y.shape[0]
    col = lambda j: pl.BlockSpec((tm, B_WIDTH), lambda i: (i, j))
    const = lambda a: pl.BlockSpec(a.shape, lambda i: (0, 0))
    g1 = _head_ones()
    return pl.pallas_call(
        _rwkv_post_kernel, grid=(t // tm,),
        in_specs=[col(0), col(0), col(2), col(0), col(0), const(r_k), const(gn_g), const(gn_b), const(g1)],
        out_specs=col(0), out_shape=jax.ShapeDtypeStruct((t, B_WIDTH), F32),
        compiler_params=_cparams(("arbitrary",)), name="rwkv_output_norm",
    )(y, shifted, shifted, kp, g, r_k, gn_g, gn_b, g1)


def _outproj_kernel(x_ref, attn_ref, rw_ref, gate_ref, wa_ref, wb_ref, wo_ref, nf_ref, wr_ref, br_ref,
                    h_ref, hn_ref, comb_ref):
    ga, gb = gate_ref[:, :D_MODEL], gate_ref[:, D_MODEL:]
    merged = ga * _dot(attn_ref[...].astype(BF16), wa_ref[...]) + gb * _dot(rw_ref[...].astype(BF16), wb_ref[...])
    h = x_ref[...] + _dot(merged.astype(BF16), wo_ref[...])
    h_ref[...] = h
    hn = _rms(h, nf_ref[...])
    hn_ref[...] = hn.astype(BF16)
    logits = _dotx(hn, wr_ref[...]) + br_ref[...]
    lane = lax.broadcasted_iota(jnp.int32, logits.shape, 1)
    big = jnp.int32(LANES)
    first = lambda hit: jnp.min(jnp.where(hit, lane, big), axis=-1, keepdims=True)
    is_g = lane < N_GROUPS
    gl = jnp.where(is_g, logits, -jnp.inf)
    gmax = jnp.max(gl, axis=-1, keepdims=True)
    p_group = 1.0 / jnp.sum(jnp.exp(gl - gmax), axis=-1, keepdims=True)
    g_sel = first(gl == gmax)
    e0 = N_GROUPS + g_sel * EXPERTS_PER_GROUP
    el = jnp.where(jnp.logical_and(lane >= e0, lane < e0 + EXPERTS_PER_GROUP), logits, -jnp.inf)
    v1 = jnp.max(el, axis=-1, keepdims=True)
    i1 = first(el == v1)
    el2 = jnp.where(lane == i1, -jnp.inf, el)
    v2 = jnp.max(el2, axis=-1, keepdims=True)
    i2 = first(el2 == v2)
    e21 = jnp.exp(v2 - v1)
    w1 = p_group / (1.0 + e21)
    w2 = p_group * e21 / (1.0 + e21)
    comb_ref[...] = jnp.where(lane == i1 - N_GROUPS, w1, 0.0) + jnp.where(lane == i2 - N_GROUPS, w2, 0.0)


def _outproj(x, attn, rw, gates, wa, wb, wo, norm_ffn, w_router, b_router, tm):
    t = x.shape[0]
    row = lambda w: pl.BlockSpec((tm, w), lambda i: (i, 0))
    const = lambda a: pl.BlockSpec(a.shape, lambda i: (0, 0))
    return pl.pallas_call(
        _outproj_kernel, grid=(t // tm,),
        in_specs=[row(D_MODEL), row(A_WIDTH), row(B_WIDTH), row(2 * D_MODEL), const(wa), const(wb), const(wo),
                  const(norm_ffn), const(w_router), const(b_router)],
        out_specs=(row(D_MODEL), row(D_MODEL), row(LANES)),
        out_shape=(jax.ShapeDtypeStruct((t, D_MODEL), F32), jax.ShapeDtypeStruct((t, D_MODEL), BF16),
                   jax.ShapeDtypeStruct((t, LANES), F32)),
        compiler_params=_cparams(("arbitrary",)), name="outproj_router",
    )(x, attn, rw, gates, wa, wb, wo, norm_ffn, w_router, b_router)


def _moe_kernel(h_ref, hn_ref, comb_ref, wg_ref, wu_ref, wd_ref, nfin_ref, o_ref, acc_ref):
    e = pl.program_id(1)
    hn = hn_ref[...]
    act = jax.nn.silu(_dot(hn, wg_ref[0])) * _dot(hn, wu_ref[0])
    y = _dot(act.astype(BF16), wd_ref[0])
    lane = lax.broadcasted_iota(jnp.int32, comb_ref.shape, 1)
    c_e = jnp.sum(jnp.where(lane == e, comb_ref[...], 0.0), axis=-1, keepdims=True)

    @pl.when(e == 0)
    def _():
        acc_ref[...] = c_e * y

    @pl.when(e > 0)
    def _():
        acc_ref[...] += c_e * y

    @pl.when(e == pl.num_programs(1) - 1)
    def _():
        o_ref[...] = _rms(h_ref[...] + acc_ref[...], nfin_ref[...])


def _moe(h, hn, comb, wg, wu, wd, norm_final, tm):
    t = h.shape[0]
    row = lambda w: pl.BlockSpec((tm, w), lambda i, e: (i, 0))
    return pl.pallas_call(
        _moe_kernel, grid=(t // tm, N_EXPERTS),
        in_specs=[row(D_MODEL), row(D_MODEL), row(LANES),
                  pl.BlockSpec((1, D_MODEL, D_EXPERT), lambda i, e: (e, 0, 0)),
                  pl.BlockSpec((1, D_MODEL, D_EXPERT), lambda i, e: (e, 0, 0)),
                  pl.BlockSpec((1, D_EXPERT, D_MODEL), lambda i, e: (e, 0, 0)),
                  pl.BlockSpec(norm_final.shape, lambda i, e: (0, 0))],
        out_specs=row(D_MODEL), out_shape=jax.ShapeDtypeStruct((t, D_MODEL), F32),
        scratch_shapes=[pltpu.VMEM((tm, D_MODEL), F32)],
        compiler_params=_cparams(("arbitrary", "arbitrary")), name="moe_experts",
    )(h, hn, comb, wg, wu, wd, norm_final)


def _tile(t, pref):
    return min(pref, t)


def _mixer_tail(x, attn, shifted, gates, state_step, weights, ffn):
    (w0, w_lora_up, a0, a_lora_up, g_lora_up, k_k, k_a, r_k, gn_g, gn_b, wa, wb, wo) = weights
    (norm_ffn, w_router, b_router, wg, wu, wd, norm_final) = ffn
    t = x.shape[0]
    tm = _tile(t, 256)
    ld, kk, kb, kp, g = _rwkv_pre(shifted, w0, w_lora_up, a0, a_lora_up, g_lora_up, k_k, k_a, tm)
    if state_step is None:
        y, st = _rwkv_seq(shifted, ld, kk, kb, kp)
        wkv = jnp.swapaxes(st, 1, 2)[None]
    else:
        v_col = shifted[:, 2 * B_WIDTH:3 * B_WIDTH].reshape(t, B_HEADS, B_HEAD_DIM, 1)
        wkv, y_col = _rwkv_step(state_step, shifted, ld, kk, kb, kp, v_col)
        y = y_col.reshape(t, B_WIDTH)
    rw = _rwkv_post(y, shifted, kp, g, r_k, gn_g, gn_b, tm)
    h, hn, comb = _outproj(x, attn, rw, gates, wa, wb, wo, norm_ffn, w_router, b_router, tm)
    out = _moe(h, hn, comb, wg, wu, wd, norm_final, _tile(t, 1024))
    return out, wkv


def kernel(x_prompt, x_sample, cache_k, cache_v, cache_idx_k, state_wkv, state_shift, page_table, norm_mix, w_in, mu_shift, w0, w_lora_up, a0, a_lora_up, g_lora_up, k_k, k_a, r_k, gn_g, gn_b, w_branch_a, w_branch_b, w_out, norm_ffn, w_router_group, b_router_group, w_router_expert, b_router_expert, w_gate, w_up, w_down, norm_final):
    assert w_in.shape[0] == 1, "single-layer kernel"
    bp, tp, _ = x_prompt.shape
    bs, ts, _ = x_sample.shape
    assert bp == 1 and ts == 1
    n_pages = page_table.shape[1]
    past = n_pages * PAGE_SIZE
    row2 = lambda a: a.reshape(1, -1)

    w_packed = _pack_w_in(w_in[0])
    mix_w = (row2(w0[0]), w_lora_up[0], row2(a0[0]), a_lora_up[0], g_lora_up[0], row2(k_k[0]), row2(k_a[0]),
             row2(r_k[0]), row2(gn_g[0]), row2(gn_b[0]),
             w_branch_a[0].astype(BF16), w_branch_b[0].astype(BF16), w_out[0].astype(BF16))
    pad_r = jnp.zeros((D_MODEL, LANES - N_GROUPS - N_EXPERTS), F32)
    w_router = jnp.concatenate([w_router_group[0], w_router_expert[0], pad_r], 1)
    b_router = jnp.concatenate([b_router_group[0], b_router_expert[0], pad_r[0]])[None]
    ffn_w = (row2(norm_ffn[0]), w_router, b_router, w_gate[0].astype(BF16), w_up[0].astype(BF16),
             w_down[0].astype(BF16), row2(norm_final))
    g_mix, mu = row2(norm_mix[0]), row2(mu_shift[0])

    xp = x_prompt[0]
    tabs = _rope_tables(jnp.arange(tp))
    (q, kf, kb16, vf, vb16, qi, kiwi, ki16, shifted, shift_last, gates) = _inproj(
        xp, g_mix, w_packed, *tabs, mu, jnp.zeros((1, SHIFT_W), F32), True, _tile(tp, 256))
    attn = _dsa_prompt(q, kb16, vb16, qi, kiwi, ki16)
    y_p, wkv_p = _mixer_tail(xp, attn, shifted, gates, None, mix_w, ffn_w)

    xs = x_sample[:, 0]
    tabs_s = _rope_tables(jnp.full((bs,), past, jnp.int32))
    (q_s, kf_s, _, vf_s, _, qi_s, kiwi_s, ki16_s, shifted_s, shift_raw_s, gates_s) = _inproj(
        xs, g_mix, w_packed, *tabs_s, mu, state_shift[0], False, _tile(bs, 256))
    pad_h = IDX_ROWS - IDX_HEADS
    qi3 = jnp.pad(qi_s.reshape(bs, IDX_HEADS, IDX_DIM), ((0, 0), (0, pad_h), (0, 0)))
    wi3 = jnp.pad(kiwi_s[:, IDX_DIM:IDX_DIM + IDX_HEADS], ((0, 0), (0, pad_h)))[:, :, None]
    scores = _dsa_sample_scores(page_table, qi3, wi3, ki16_s[:, None, :], jnp.swapaxes(cache_idx_k, 2, 3))
    k_sel = min(TOPK_MAX, (past + ts) // 4)
    mask = _topk_mask(scores.reshape(bs, -1), k_sel).reshape(scores.shape)
    cols = lambda a: jnp.swapaxes(a.reshape(bs, A_HEADS, HEAD_DIM), 1, 2)
    page_t = lambda cache: jnp.transpose(cache, (0, 1, 3, 4, 2))
    attn_t = _dsa_sample_attn(page_table, cols(q_s), cols(kf_s), cols(vf_s), mask, page_t(cache_k), page_t(cache_v))
    attn_s = jnp.swapaxes(attn_t[:, :, :A_HEADS], 1, 2).reshape(bs, A_WIDTH)
    y_s, wkv_s = _mixer_tail(xs, attn_s, shifted_s, gates_s, state_wkv[0], mix_w, ffn_w)

    kv5 = lambda a, b, t: a.reshape(1, b, t, A_HEADS, HEAD_DIM)
    return (y_p[None], y_s[:, None],
            kv5(kf, 1, tp), kv5(vf, 1, tp), kiwi[:, :IDX_DIM].reshape(1, 1, tp, IDX_DIM),
            wkv_p[None], shift_last.reshape(1, 1, SHIFT_W),
            kv5(kf_s, bs, 1), kv5(vf_s, bs, 1), kiwi_s[:, :IDX_DIM].reshape(1, bs, 1, IDX_DIM),
            wkv_s[None], shift_raw_s[None])
```

```python
import functools

import jax
import jax.numpy as jnp
import numpy as np
from jax import lax
from jax.experimental import pallas as pl
from jax.experimental.pallas import tpu as pltpu

F32 = jnp.float32
BF16 = jnp.bfloat16

D_MODEL = 1024
PAGE_SIZE = 128
A_HEADS = 8
HEAD_DIM = 64
A_WIDTH = A_HEADS * HEAD_DIM
IDX_HEADS = 4
IDX_DIM = 64
TOPK_MAX = 256
ROT_DIM = HEAD_DIM // 4
ROPE_THETA = 500000.0
B_HEADS = 8
B_HEAD_DIM = 64
B_WIDTH = B_HEADS * B_HEAD_DIM
DECAY_LORA = 64
AAA_LORA = 64
GATE_LORA = 128
GN_EPS = B_HEAD_DIM * 1e-5
SHIFT_W = 3 * B_WIDTH + DECAY_LORA + AAA_LORA + GATE_LORA
N_GROUPS = 4
EXPERTS_PER_GROUP = 8
N_EXPERTS = N_GROUPS * EXPERTS_PER_GROUP
D_EXPERT = 512
RMS_EPS = 1e-6

LANES = 128
SUBLANES = 8
VMEM_LIMIT_BYTES = 56 * 1024 * 1024

QKV_END = 3 * A_WIDTH
QI_END = QKV_END + IDX_HEADS * IDX_DIM
KIWI_END = QI_END + LANES
SHIFT_END = KIWI_END + SHIFT_W
D_IN_PACKED = SHIFT_END + 2 * D_MODEL
IDX_W_SCALE = IDX_HEADS ** -0.5 * IDX_DIM ** -0.5
NEG_BIG = -1e30
CHUNK = 64


def _cparams(sem):
    return pltpu.CompilerParams(dimension_semantics=sem, vmem_limit_bytes=VMEM_LIMIT_BYTES)


def _dot(a, b):
    return jnp.dot(a, b, preferred_element_type=F32)


def _dot_t(a, b):
    return lax.dot_general(a, b, (((1,), (1,)), ((), ())), preferred_element_type=F32)


def _dotx(a, b):
    return jnp.dot(a, b, preferred_element_type=F32, precision=lax.Precision.HIGHEST)


def _dotx_t(a, b):
    return lax.dot_general(a, b, (((1,), (1,)), ((), ())), preferred_element_type=F32,
                           precision=lax.Precision.HIGHEST)


def _split2(a):
    hi = a.astype(BF16)
    return hi, (a - hi.astype(F32)).astype(BF16)


def _dot3(a, b):
    ah, al = _split2(a)
    bh, bl = _split2(b)
    return _dot(ah, bh) + (_dot(ah, bl) + _dot(al, bh))


def _dot3_t(a, b):
    ah, al = _split2(a)
    bh, bl = _split2(b)
    return _dot_t(ah, bh) + (_dot_t(ah, bl) + _dot_t(al, bh))


def _rms(x, g):
    return x * lax.rsqrt(jnp.mean(x * x, axis=-1, keepdims=True) + RMS_EPS) * g


def _rope_slab(x, c, sa, sb):
    return x * c + pltpu.roll(x, ROT_DIM // 2, 1) * sa + pltpu.roll(x, LANES - ROT_DIM // 2, 1) * sb


def _inproj_kernel(seq_mode, x_ref, g_ref, w_ref, c_ref, sa_ref, sb_ref, mu_ref, prev_ref,
                   q_ref, kf_ref, kb_ref, vf_ref, vb_ref, qi_ref, kiwi_ref, kib_ref, sh_ref, last_ref,
                   gate_ref, carry_ref):
    tm = x_ref.shape[0]
    xn = _rms(x_ref[...], g_ref[...]).astype(BF16)
    c, sa, sb = c_ref[...], sa_ref[...], sb_ref[...]

    def proj(c0, c1):
        return _dot(xn, w_ref[:, c0:c1])

    def rope(z):
        return jnp.concatenate(
            [_rope_slab(z[:, s:s + LANES], c, sa, sb) for s in range(0, z.shape[1], LANES)], axis=1)

    q = rope(proj(0, A_WIDTH)) * (HEAD_DIM ** -0.5)
    q_ref[...] = q.astype(BF16)
    k = rope(proj(A_WIDTH, 2 * A_WIDTH))
    kf_ref[...] = k
    kb_ref[...] = k.astype(BF16)
    v = proj(2 * A_WIDTH, QKV_END)
    vf_ref[...] = v
    vb_ref[...] = v.astype(BF16)
    qi_ref[...] = rope(proj(QKV_END, QI_END)).astype(BF16)
    kiwi = proj(QI_END, KIWI_END)
    lane = lax.broadcasted_iota(jnp.int32, kiwi.shape, 1)
    kiwi = jnp.where(lane < IDX_DIM, _rope_slab(kiwi, c, sa, sb), kiwi * IDX_W_SCALE)
    kiwi_ref[...] = kiwi
    kib_ref[...] = kiwi[:, :IDX_DIM].astype(BF16)

    u = proj(KIWI_END, SHIFT_END)
    if seq_mode:
        @pl.when(pl.program_id(0) == 0)
        def _():
            carry_ref[...] = prev_ref[...]
        row = lax.broadcasted_iota(jnp.int32, u.shape, 0)
        u_prev = jnp.where(row == 0, carry_ref[...], pltpu.roll(u, 1, 0))
        carry_ref[...] = u[tm - 1:tm, :]
    else:
        u_prev = prev_ref[...]
    sh_ref[...] = u + (u_prev - u) * mu_ref[...]
    last_ref[...] = u[tm - 1:tm, :] if seq_mode else u
    gate_ref[...] = jax.nn.sigmoid(proj(SHIFT_END, D_IN_PACKED))


def _inproj(x, norm_g, w_packed, rope_c, rope_sa, rope_sb, mu, prev, seq_mode, tm):
    t = x.shape[0]
    assert t % tm == 0
    row = lambda w: pl.BlockSpec((tm, w), lambda i: (i, 0))
    const = lambda a: pl.BlockSpec(a.shape, lambda i: (0, 0))
    prev_spec = const(prev) if seq_mode else row(SHIFT_W)
    out_shape = (
        jax.ShapeDtypeStruct((t, A_WIDTH), BF16),
        jax.ShapeDtypeStruct((t, A_WIDTH), F32), jax.ShapeDtypeStruct((t, A_WIDTH), BF16),
        jax.ShapeDtypeStruct((t, A_WIDTH), F32), jax.ShapeDtypeStruct((t, A_WIDTH), BF16),
        jax.ShapeDtypeStruct((t, IDX_HEADS * IDX_DIM), BF16),
        jax.ShapeDtypeStruct((t, LANES), F32),
        jax.ShapeDtypeStruct((t, IDX_DIM), BF16),
        jax.ShapeDtypeStruct((t, SHIFT_W), F32),
        jax.ShapeDtypeStruct((1 if seq_mode else t, SHIFT_W), F32),
        jax.ShapeDtypeStruct((t, 2 * D_MODEL), F32),
    )
    out_specs = (row(A_WIDTH), row(A_WIDTH), row(A_WIDTH), row(A_WIDTH), row(A_WIDTH),
                 row(IDX_HEADS * IDX_DIM), row(LANES), row(IDX_DIM), row(SHIFT_W),
                 pl.BlockSpec((1, SHIFT_W), lambda i: (0, 0)) if seq_mode else row(SHIFT_W), row(2 * D_MODEL))
    return pl.pallas_call(
        functools.partial(_inproj_kernel, seq_mode),
        grid=(t // tm,),
        in_specs=[row(D_MODEL), const(norm_g), const(w_packed), row(LANES), row(LANES), row(LANES),
                  const(mu), prev_spec],
        out_specs=out_specs, out_shape=out_shape,
        scratch_shapes=[pltpu.VMEM((1, SHIFT_W), F32)],
        compiler_params=_cparams(("arbitrary",)),
        name="inproj_seq" if seq_mode else "inproj_batch",
    )(x, norm_g, w_packed, rope_c, rope_sa, rope_sb, mu, prev)


def _rope_tables(pos):
    half = ROT_DIM // 2
    inv = ROPE_THETA ** (-jnp.arange(half, dtype=F32) / half)
    ang = pos.astype(F32)[:, None] * inv[None, :]
    cos, sin = jnp.cos(ang), jnp.sin(ang)
    t = pos.shape[0]
    pad = jnp.zeros((t, HEAD_DIM - ROT_DIM), F32)
    zero = jnp.zeros((t, half), F32)
    c = jnp.concatenate([cos, cos, pad + 1.0], 1)
    sa = jnp.concatenate([zero, sin, pad], 1)
    sb = jnp.concatenate([-sin, zero, pad], 1)
    rep = lambda a: jnp.concatenate([a] * (LANES // HEAD_DIM), 1)
    return rep(c), rep(sa), rep(sb)


def _pack_w_in(w_in):
    lead = QI_END + IDX_DIM + IDX_HEADS
    pad = jnp.zeros((D_MODEL, KIWI_END - lead), w_in.dtype)
    return jnp.concatenate([w_in[:, :lead], pad, w_in[:, lead:]], 1).astype(BF16)


DSA_TQ = 256
DSA_TK = 512
THR_ROWS = 128
THR_WARM_STEPS = 10
THR_CHEAP_STEPS = 1
THR_UNROLL = DSA_TK // LANES
THR_SLOTS = 3
THR_HALF = 64


def _causal_pairs(t, tq, tk):
    qi, kj, last = [], [], []
    for i in range(t // tq):
        nk = (i * tq + tq - 1) // tk + 1
        for j in range(nk):
            qi.append(i), kj.append(j), last.append(int(j == nk - 1))
    return (jnp.asarray(qi, jnp.int32), jnp.asarray(kj, jnp.int32), jnp.asarray(last, jnp.int32))


def _lane_tile(x, width):
    return jnp.concatenate([x] * (width // LANES), axis=1)


def _index_scores(qi, ki, wi, q0, k0):
    tq, tk = qi.shape[0], ki.shape[0]
    s = None
    for h in range(IDX_HEADS):
        sh = jnp.maximum(_dot_t(qi[:, h * IDX_DIM:(h + 1) * IDX_DIM], ki), 0.0)
        sh = sh * _lane_tile(wi[:, h * LANES:(h + 1) * LANES], tk)
        s = sh if s is None else s + sh
    qpos = q0 + lax.broadcasted_iota(jnp.int32, (tq, tk), 0)
    kpos = k0 + lax.broadcasted_iota(jnp.int32, (tq, tk), 1)
    return jnp.where(kpos <= qpos, s, -jnp.inf)


def _kth_largest_rows(sc_ref, r0, ncol, kf):
    n_rows = kf.shape[0]
    half = min(THR_HALF, n_rows)
    rows = pl.ds(r0, n_rows)
    shape = (n_rows, LANES)

    def col_reduce(fn, init, combine):
        def body(c, acc):
            for u in range(THR_UNROLL):
                blk = sc_ref[rows, pl.ds(pl.multiple_of((c * THR_UNROLL + u) * LANES, LANES), LANES)]
                acc = combine(acc, fn(blk))
            return acc
        return lax.fori_loop(0, ncol // THR_UNROLL, body, jnp.full(shape, init, F32))

    def count_where(reduce, pred):
        return jnp.sum(reduce(lambda b: jnp.where(pred(b), 1.0, 0.0), 0.0, jnp.add), axis=-1, keepdims=True)

    def count_gt(reduce, m):
        mb = jnp.broadcast_to(m, shape)
        return count_where(reduce, lambda b: b > mb)

    def snap(reduce, lx, hi):
        lb, hb = jnp.broadcast_to(lx, shape), jnp.broadcast_to(hi, shape)
        lo_acc = reduce(lambda b: jnp.where(b > lb, b, jnp.inf), jnp.inf, jnp.minimum)
        hi_acc = reduce(lambda b: jnp.where(b <= hb, b, -jnp.inf), -jnp.inf, jnp.maximum)
        return jnp.min(lo_acc, axis=-1, keepdims=True), jnp.max(hi_acc, axis=-1, keepdims=True)

    def bisect(reduce, k, bounds):
        lx, hi, a, b, c_hi = bounds
        m = 0.5 * a + 0.5 * b
        c = count_gt(reduce, m)
        up = c >= k
        return (jnp.where(up, m, lx), jnp.where(up, hi, m), jnp.where(up, m, a), jnp.where(up, b, m),
                jnp.where(up, c_hi, c))

    def search(reduce, k, lx, hi):
        def round_(state):
            lx, hi, _, _ = state
            d_lo, d_hi = snap(reduce, lx, hi)
            m = 0.5 * d_lo + 0.5 * d_hi
            m = jnp.where(m < d_hi, m, d_lo)
            up = count_gt(reduce, m) >= k
            lx, hi = jnp.where(up, m, lx), jnp.where(up, hi, m)
            bounds = (lx, hi, jnp.maximum(lx, d_lo), jnp.minimum(hi, d_hi), jnp.zeros_like(lx))
            for _ in range(THR_CHEAP_STEPS):
                bounds = bisect(reduce, k, bounds)
            open_rows = jnp.max(jnp.where(d_lo < d_hi, 1.0, 0.0))
            return bounds[0], bounds[1], d_lo, open_rows
        init = (lx, hi, jnp.zeros((n_rows, 1), F32), jnp.float32(1.0))
        return lax.while_loop(lambda s: s[3] > 0.0, round_, init)[2]

    lx = jnp.full((n_rows, 1), -jnp.inf, F32)
    hi = jnp.full((n_rows, 1), jnp.inf, F32)
    d_lo, d_hi = snap(col_reduce, lx, hi)
    lx, hi, _, _, c_hi = lax.fori_loop(0, THR_WARM_STEPS, lambda _, bnd: bisect(col_reduce, kf, bnd),
                                       (lx, hi, d_lo, d_hi, jnp.zeros((n_rows, 1), F32)))

    def compact(h0):
        hrows = pl.ds(r0 + h0, half)
        hshape = (half, LANES)
        lb = jnp.broadcast_to(lx[h0:h0 + half], hshape)
        hb = jnp.broadcast_to(hi[h0:h0 + half], hshape)

        def body(c, carry):
            cnt, bufs = carry[0], list(carry[1:])
            for u in range(THR_UNROLL):
                v = sc_ref[hrows, pl.ds(pl.multiple_of((c * THR_UNROLL + u) * LANES, LANES), LANES)]
                inside = jnp.where(v > lb, v, jnp.inf) <= hb
                key = jnp.where(inside, cnt, -1.0)
                bufs = [jnp.where(key == float(slot), v, buf) for slot, buf in enumerate(bufs)]
                cnt = cnt + jnp.where(inside, 1.0, 0.0)
            return (cnt, *bufs)
        empty = jnp.full(hshape, -jnp.inf, F32)
        return lax.fori_loop(0, ncol // THR_UNROLL, body, (jnp.zeros(hshape, F32),) + (empty,) * THR_SLOTS)

    halves = [compact(h0) for h0 in range(0, n_rows, half)]
    cnt = jnp.concatenate([h[0] for h in halves], axis=0)
    bufs = [jnp.concatenate([h[1 + slot] for h in halves], axis=0) for slot in range(THR_SLOTS)]

    def small_reduce(fn, init, combine):
        acc = jnp.full(shape, init, F32)
        for buf in bufs:
            acc = combine(acc, fn(buf))
        return acc

    def from_slots():
        thr = search(small_reduce, kf - c_hi, lx, hi)
        tb = jnp.broadcast_to(thr, shape)
        return (thr, c_hi + count_where(small_reduce, lambda b: b > tb),
                c_hi + count_where(small_reduce, lambda b: b >= tb))

    def from_all():
        thr = search(col_reduce, kf, lx, hi)
        tb = jnp.broadcast_to(thr, shape)
        return thr, count_where(col_reduce, lambda b: b > tb), count_where(col_reduce, lambda b: b >= tb)

    return lax.cond(jnp.max(cnt) > float(THR_SLOTS), from_all, from_slots)


def _tie_cutoff_rows(sc_ref, r0, ncol, thr, need):
    n_rows = thr.shape[0]
    rows = pl.ds(r0, n_rows)
    shape = (n_rows, LANES)
    tb = jnp.broadcast_to(thr, shape)
    lane = lax.broadcasted_iota(jnp.int32, shape, 1)

    def count_le(j):
        jb = jnp.broadcast_to(j, shape)

        def body(c, acc):
            blk = sc_ref[rows, pl.ds(pl.multiple_of(c * LANES, LANES), LANES)]
            hit = jnp.logical_and(blk == tb, lane + c * LANES <= jb)
            return acc + jnp.where(hit, 1.0, 0.0)
        return jnp.sum(lax.fori_loop(0, ncol, body, jnp.zeros(shape, F32)), axis=-1, keepdims=True)

    def step(_, lh):
        lo, hi = lh
        mid = (lo + hi) // 2
        ok = count_le(mid) >= need
        return jnp.where(ok, lo, mid), jnp.where(ok, mid, hi)

    lo = jnp.full((n_rows, 1), -1, jnp.int32)
    hi = jnp.full((n_rows, 1), 1, jnp.int32) * (ncol * LANES - 1)
    n_steps = int(np.ceil(np.log2(sc_ref.shape[1] + 1))) + 1
    _, hi = lax.fori_loop(0, n_steps, step, (lo, hi))
    return hi


def _dsa_thr_kernel(qi_blk, kj_blk, last_blk, qi_ref, wi_ref, ki_ref, thr_ref, cut_ref, sc_ref):
    p = pl.program_id(0)
    i, j = qi_blk[p], kj_blk[p]
    tq, tk = qi_ref.shape[0], ki_ref.shape[0]
    s = _index_scores(qi_ref[...], ki_ref[...], wi_ref[...], i * tq, j * tk)
    sc_ref[:, pl.ds(pl.multiple_of(j * tk, tk), tk)] = s

    @pl.when(last_blk[p] == 1)
    def _():
        ncol = (j + 1) * (tk // LANES)
        for g in range(tq // THR_ROWS):
            r0 = g * THR_ROWS
            qpos = i * tq + r0 + lax.broadcasted_iota(jnp.int32, (THR_ROWS, 1), 0)
            kf = jnp.minimum(qpos + 1, TOPK_MAX).astype(F32)
            thr, n_gt, n_ge = _kth_largest_rows(sc_ref, r0, ncol, kf)
            thr_ref[r0:r0 + THR_ROWS, :] = jnp.broadcast_to(thr, (THR_ROWS, LANES))
            cut_ref[r0:r0 + THR_ROWS, :] = jnp.full((THR_ROWS, LANES), sc_ref.shape[1], jnp.int32)

            @pl.when(jnp.max(n_ge - kf) > 0.0)
            def _():
                cut = _tie_cutoff_rows(sc_ref, r0, ncol, thr, kf - n_gt)
                cut = jnp.where(n_ge > kf, cut, sc_ref.shape[1])
                cut_ref[r0:r0 + THR_ROWS, :] = jnp.broadcast_to(cut, (THR_ROWS, LANES))


ATTN_ROWS = 64


def _dsa_attn_kernel(qi_blk, kj_blk, last_blk, q_ref, qi_ref, wi_ref, thr_ref, cut_ref, ki_ref, k_ref, vx_ref,
                     o_ref, m_ref, acc_ref, bias_ref, s_ref, p_ref, alpha_ref):
    p = pl.program_id(0)
    i, j = qi_blk[p], kj_blk[p]
    tq, tk = q_ref.shape[1], k_ref.shape[1]

    @pl.when(j == 0)
    def _():
        m_ref[...] = jnp.full(m_ref.shape, NEG_BIG, F32)
        acc_ref[...] = jnp.zeros(acc_ref.shape, F32)

    s = _index_scores(qi_ref[...], ki_ref[...], wi_ref[...], i * tq, j * tk)
    thr = _lane_tile(thr_ref[...], tk)
    kpos = j * tk + lax.broadcasted_iota(jnp.int32, (tq, tk), 1)
    sel = jnp.logical_or(s > thr, jnp.logical_and(s == thr, kpos <= _lane_tile(cut_ref[...], tk)))
    bias_ref[...] = jnp.where(sel, 0.0, NEG_BIG)

    def logits(h):
        s_ref[h % 2] = _dot_t(q_ref[h], k_ref[h])

    def softmax(h):
        slot = h % 2
        for r0 in range(0, tq, ATTN_ROWS):
            rows = slice(r0, r0 + ATTN_ROWS)
            x = s_ref[slot, rows, :] + bias_ref[rows, :]
            m_prev = m_ref[h, rows, :]
            m_new = jnp.maximum(m_prev, jnp.max(x, axis=-1, keepdims=True))
            p_ref[slot, rows, :] = jnp.exp(x - _lane_tile(m_new, tk)).astype(BF16)
            alpha_ref[slot, rows, :] = jnp.exp(m_prev - m_new)
            m_ref[h, rows, :] = m_new

    def values(h):
        acc_ref[h] = alpha_ref[h % 2] * acc_ref[h] + _dot(p_ref[h % 2], vx_ref[h])

    logits(0)
    logits(1)
    softmax(0)

    def head_step(h, carry):
        values(h - 2)
        softmax(h - 1)
        logits(h)
        return carry
    lax.fori_loop(2, A_HEADS, head_step, 0)
    softmax(A_HEADS - 1)
    values(A_HEADS - 2)
    values(A_HEADS - 1)

    @pl.when(last_blk[p] == 1)
    def _():
        o_ref[...] = jnp.concatenate(
            [acc_ref[h][:, 0:HEAD_DIM] / acc_ref[h][:, HEAD_DIM:HEAD_DIM + 1] for h in range(A_HEADS)], axis=1)


def _dsa_prompt(q, k, v, qi, kiwi, ki):
    t = q.shape[0]
    tq, tk = min(DSA_TQ, t), min(DSA_TK, t)
    assert t % tq == 0 and t % tk == 0 and tq % THR_ROWS == 0
    kiwi = jnp.repeat(kiwi[:, IDX_DIM:IDX_DIM + IDX_HEADS], LANES, axis=1)
    pairs = _causal_pairs(t, tq, tk)
    n_pairs = pairs[0].shape[0]
    qrow = lambda w: pl.BlockSpec((tq, w), lambda p, qb, kb, lb: (qb[p], 0))
    krow = lambda w: pl.BlockSpec((tk, w), lambda p, qb, kb, lb: (kb[p], 0))
    thr, cut = pl.pallas_call(
        _dsa_thr_kernel,
        grid_spec=pltpu.PrefetchScalarGridSpec(
            num_scalar_prefetch=3, grid=(n_pairs,),
            in_specs=[qrow(IDX_HEADS * IDX_DIM), qrow(IDX_HEADS * LANES), krow(IDX_DIM)],
            out_specs=(qrow(LANES), qrow(LANES)),
            scratch_shapes=[pltpu.VMEM((tq, t), F32)]),
        out_shape=(jax.ShapeDtypeStruct((t, LANES), F32), jax.ShapeDtypeStruct((t, LANES), jnp.int32)),
        compiler_params=_cparams(("arbitrary",)), name="dsa_prompt_threshold",
    )(*pairs, qi, kiwi, ki)
    qhead = pl.BlockSpec((A_HEADS, tq, HEAD_DIM), lambda p, qb, kb, lb: (0, qb[p], 0))
    khead = lambda w: pl.BlockSpec((A_HEADS, tk, w), lambda p, qb, kb, lb: (0, kb[p], 0))
    heads = lambda a: jnp.swapaxes(a.reshape(t, A_HEADS, HEAD_DIM), 0, 1)
    return pl.pallas_call(
        _dsa_attn_kernel,
        grid_spec=pltpu.PrefetchScalarGridSpec(
            num_scalar_prefetch=3, grid=(n_pairs,),
            in_specs=[qhead, qrow(IDX_HEADS * IDX_DIM), qrow(IDX_HEADS * LANES), qrow(LANES), qrow(LANES),
                      krow(IDX_DIM), khead(HEAD_DIM), khead(LANES)],
            out_specs=qrow(A_WIDTH),
            scratch_shapes=[pltpu.VMEM((A_HEADS, tq, LANES), F32), pltpu.VMEM((A_HEADS, tq, LANES), F32),
                            pltpu.VMEM((tq, tk), F32), pltpu.VMEM((2, tq, tk), F32),
                            pltpu.VMEM((2, tq, tk), BF16), pltpu.VMEM((2, tq, LANES), F32)]),
        out_shape=jax.ShapeDtypeStruct((t, A_WIDTH), F32),
        compiler_params=_cparams(("arbitrary",)), name="dsa_prompt_attention",
    )(*pairs, heads(q), qi, kiwi, thr, cut, ki, heads(k), _values_with_ones(heads(v)))


def _values_with_ones(v3):
    ones = jnp.ones(v3.shape[:2] + (1,), v3.dtype)
    pad = jnp.zeros(v3.shape[:2] + (LANES - HEAD_DIM - 1,), v3.dtype)
    return jnp.concatenate([v3, ones, pad], axis=-1)


IDX_ROWS = 16


def _dsa_sample_scores_kernel(pt_ref, qi_ref, wi_ref, kin_ref, cache_ref, o_ref, buf_ref, sem_ref):
    b, nb = pl.program_id(0), pl.num_programs(0)
    n_pages = pt_ref.shape[1]
    past = n_pages * PAGE_SIZE

    def issue(bb, slot):
        for p in range(n_pages):
            pltpu.make_async_copy(cache_ref.at[0, pt_ref[bb, p]], buf_ref.at[slot, p], sem_ref.at[slot]).start()

    @pl.when(b == 0)
    def _():
        issue(0, 0)

    @pl.when(b + 1 < nb)
    def _():
        issue(b + 1, (b + 1) % 2)

    slot = b % 2
    pltpu.make_async_copy(buf_ref.at[slot], buf_ref.at[slot], sem_ref.at[slot]).wait()
    qi, wi = qi_ref[0], wi_ref[0]

    def pages(g, carry):
        rows = []
        for u in range(SUBLANES):
            s = jnp.maximum(_dot(qi, buf_ref[slot, g * SUBLANES + u].astype(BF16)), 0.0) * wi
            rows.append(jnp.sum(s, axis=0, keepdims=True))
        o_ref[0, pl.ds(pl.multiple_of(g * SUBLANES, SUBLANES), SUBLANES), :] = jnp.concatenate(rows, axis=0)
        return carry
    lax.fori_loop(0, n_pages // SUBLANES, pages, 0)
    kin = kin_ref[0].astype(F32)
    s_new = jnp.maximum(jnp.sum(qi.astype(F32) * kin, axis=-1, keepdims=True), 0.0) * wi
    s_new = jnp.sum(s_new, axis=0, keepdims=True)
    lane = lax.broadcasted_iota(jnp.int32, (SUBLANES, LANES), 1)
    row = lax.broadcasted_iota(jnp.int32, (SUBLANES, LANES), 0)
    o_ref[0, n_pages:n_pages + SUBLANES, :] = jnp.where(jnp.logical_and(lane == 0, row == 0), s_new, -jnp.inf)


def _dsa_sample_scores(page_table, qi3, wi3, ki_new, cache_idx_t):
    bsz, n_pages = page_table.shape
    assert n_pages % SUBLANES == 0
    n_rows = n_pages + SUBLANES
    blk = lambda a: pl.BlockSpec((1,) + a.shape[1:], lambda b, pt: (b, 0, 0))
    return pl.pallas_call(
        _dsa_sample_scores_kernel,
        grid_spec=pltpu.PrefetchScalarGridSpec(
            num_scalar_prefetch=1, grid=(bsz,),
            in_specs=[blk(qi3), blk(wi3), blk(ki_new), pl.BlockSpec(memory_space=pl.ANY)],
            out_specs=pl.BlockSpec((1, n_rows, LANES), lambda b, pt: (b, 0, 0)),
            scratch_shapes=[pltpu.VMEM((2, n_pages, IDX_DIM, PAGE_SIZE), F32), pltpu.SemaphoreType.DMA((2,))]),
        out_shape=jax.ShapeDtypeStruct((bsz, n_rows, LANES), F32),
        compiler_params=_cparams(("arbitrary",)), name="dsa_sample_scores",
    )(page_table, qi3, wi3, ki_new, cache_idx_t)


def _topk_mask_kernel(k_sel, sc_ref, mask_ref):
    bsz, width = sc_ref.shape
    ncol = width // LANES + 0 * pl.program_id(0)
    shape = (bsz, LANES)
    kf = jnp.full((bsz, 1), float(k_sel), F32)
    thr, n_gt, n_ge = _kth_largest_rows(sc_ref, 0, ncol, kf)
    cut = lax.cond(jnp.max(n_ge - kf) > 0.0,
                   lambda: jnp.where(n_ge > kf, _tie_cutoff_rows(sc_ref, 0, ncol, thr, kf - n_gt), width),
                   lambda: jnp.full((bsz, 1), width, jnp.int32))
    tb, cb = jnp.broadcast_to(thr, shape), jnp.broadcast_to(cut, shape)
    lane = lax.broadcasted_iota(jnp.int32, shape, 1)

    def mark(c, carry):
        cols = pl.ds(pl.multiple_of(c * LANES, LANES), LANES)
        v = sc_ref[:, cols]
        sel = jnp.logical_or(v > tb, jnp.logical_and(v == tb, lane + c * LANES <= cb))
        mask_ref[:, cols] = jnp.where(sel, 1.0, 0.0)
        return carry
    lax.fori_loop(0, ncol, mark, 0)


def _topk_mask(scores, k_sel):
    bsz, width = scores.shape
    assert width % DSA_TK == 0
    whole = pl.BlockSpec((bsz, width), lambda i: (0, 0))
    return pl.pallas_call(
        functools.partial(_topk_mask_kernel, k_sel), grid=(1,),
        in_specs=[whole], out_specs=whole,
        out_shape=jax.ShapeDtypeStruct((bsz, width), F32),
        compiler_params=_cparams(("arbitrary",)), name="dsa_sample_topk",
    )(scores)


SAMPLE_PAGES_PER_STEP = 8


def _dsa_sample_attn_kernel(pt_ref, qt_ref, knt_ref, vnt_ref, mask_ref, ck_ref, cv_ref, o_ref,
                            kbuf, vbuf, sem_ref, m_ref, l_ref, acc_ref):
    b, c = pl.program_id(0), pl.program_id(1)
    nb, nc = pl.num_programs(0), pl.num_programs(1)
    pps = SAMPLE_PAGES_PER_STEP
    n_pages = pt_ref.shape[1]
    step = b * nc + c

    def issue(s, slot):
        bb, cc = s // nc, s % nc
        for i in range(pps):
            page = pt_ref[bb, cc * pps + i]
            pltpu.make_async_copy(ck_ref.at[0, page], kbuf.at[slot, i], sem_ref.at[0, slot]).start()
            pltpu.make_async_copy(cv_ref.at[0, page], vbuf.at[slot, i], sem_ref.at[1, slot]).start()

    @pl.when(step == 0)
    def _():
        issue(0, 0)

    @pl.when(step + 1 < nb * nc)
    def _():
        issue(step + 1, (step + 1) % 2)

    slot = step % 2
    pltpu.make_async_copy(kbuf.at[slot], kbuf.at[slot], sem_ref.at[0, slot]).wait()
    pltpu.make_async_copy(vbuf.at[slot], vbuf.at[slot], sem_ref.at[1, slot]).wait()

    @pl.when(c == 0)
    def _():
        m_ref[...] = jnp.full(m_ref.shape, NEG_BIG, F32)
        l_ref[...] = jnp.zeros(l_ref.shape, F32)
        acc_ref[...] = jnp.zeros(acc_ref.shape, F32)

    keep = mask_ref[0, pl.ds(pl.multiple_of(c * pps, pps), pps), :] > 0.0
    qt = qt_ref[0].astype(F32)
    heads = range(A_HEADS)
    qcols = [jnp.broadcast_to(qt[:, h:h + 1], (HEAD_DIM, PAGE_SIZE)) for h in heads]
    lg = [jnp.concatenate([jnp.sum(kbuf[slot, i, h] * qcols[h], axis=0, keepdims=True) for i in range(pps)], axis=0)
          for h in heads]
    lg = [jnp.where(keep, x, NEG_BIG) for x in lg]
    m_old = [m_ref[h] for h in heads]
    m_new = [jnp.maximum(m_old[h], jnp.max(lg[h], axis=(0, 1), keepdims=True)) for h in heads]
    alpha = [jnp.exp(m_old[h] - m_new[h]) for h in heads]
    p = [jnp.where(keep, jnp.exp(lg[h] - m_new[h][0:1, :]), 0.0) for h in heads]
    for h in heads:
        l_ref[h] = alpha[h] * l_ref[h] + jnp.sum(p[h], axis=(0, 1), keepdims=True)
        m_ref[h] = m_new[h]
    for h in heads:
        pv = vbuf[slot, 0, h] * p[h][0:1, :]
        for i in range(1, pps):
            pv = pv + vbuf[slot, i, h] * p[h][i:i + 1, :]
        acc_ref[h] = alpha[h][0:1, :] * acc_ref[h] + pv

    @pl.when(c == nc - 1)
    def _():
        lane = lax.broadcasted_iota(jnp.int32, (HEAD_DIM, LANES), 1)
        keep_new = mask_ref[0, n_pages:n_pages + 1, 0:1] > 0.0
        out = jnp.zeros((HEAD_DIM, LANES), F32)
        for h in range(A_HEADS):
            lg_new = jnp.sum(knt_ref[0][:, h:h + 1] * qt[:, h:h + 1], axis=0, keepdims=True)
            lg_new = jnp.where(keep_new, lg_new, NEG_BIG)
            m_old = m_ref[h][0:1, 0:1]
            m_new = jnp.maximum(m_old, lg_new)
            alpha = jnp.exp(m_old - m_new)
            p_new = jnp.where(keep_new, jnp.exp(lg_new - m_new), 0.0)
            l_fin = alpha * l_ref[h][0:1, 0:1] + p_new
            col = alpha * jnp.sum(acc_ref[h], axis=-1, keepdims=True) + p_new * vnt_ref[0][:, h:h + 1]
            out = jnp.where(lane == h, col / l_fin, out)
        o_ref[0] = out


def _dsa_sample_attn(page_table, q_t, k_new_t, v_new_t, mask3, cache_k_t, cache_v_t):
    bsz, n_pages = page_table.shape
    pps = SAMPLE_PAGES_PER_STEP
    assert n_pages % pps == 0
    col = lambda: pl.BlockSpec((1, HEAD_DIM, A_HEADS), lambda b, c, pt: (b, 0, 0))
    page_buf = pltpu.VMEM((2, pps, A_HEADS, HEAD_DIM, PAGE_SIZE), F32)
    return pl.pallas_call(
        _dsa_sample_attn_kernel,
        grid_spec=pltpu.PrefetchScalarGridSpec(
            num_scalar_prefetch=1, grid=(bsz, n_pages // pps),
            in_specs=[col(), col(), col(), pl.BlockSpec((1,) + mask3.shape[1:], lambda b, c, pt: (b, 0, 0)),
                      pl.BlockSpec(memory_space=pl.ANY), pl.BlockSpec(memory_space=pl.ANY)],
            out_specs=pl.BlockSpec((1, HEAD_DIM, LANES), lambda b, c, pt: (b, 0, 0)),
            scratch_shapes=[page_buf, page_buf, pltpu.SemaphoreType.DMA((2, 2)),
                            pltpu.VMEM((A_HEADS, SUBLANES, LANES), F32), pltpu.VMEM((A_HEADS, SUBLANES, LANES), F32),
                            pltpu.VMEM((A_HEADS, HEAD_DIM, LANES), F32)]),
        out_shape=jax.ShapeDtypeStruct((bsz, HEAD_DIM, LANES), F32),
        compiler_params=_cparams(("arbitrary", "arbitrary")), name="dsa_sample_attention",
    )(page_table, q_t, k_new_t, v_new_t, mask3, cache_k_t, cache_v_t)


def _head_ones():
    h = np.arange(B_WIDTH) // B_HEAD_DIM
    return jnp.asarray((h[:, None] == h[None, :]).astype(np.float32))


def _rwkv_pre_kernel(sh_ref, w0_ref, wup_ref, a0_ref, aup_ref, gup_ref, kk_ref_, ka_ref, g1_ref,
                     ld_ref, kk_ref, kb_ref, kp_ref, g_ref):
    k = sh_ref[:, B_WIDTH:2 * B_WIDTH]
    o = 3 * B_WIDTH
    xw = sh_ref[:, o:o + DECAY_LORA]
    xa = sh_ref[:, o + DECAY_LORA:o + DECAY_LORA + AAA_LORA]
    xg = sh_ref[:, o + DECAY_LORA + AAA_LORA:SHIFT_W]
    w = w0_ref[...] + _dotx(jnp.tanh(xw), wup_ref[...])
    logw = -jax.nn.softplus(-w) - 0.5
    ld_ref[...] = -jnp.exp(logw)
    a = jax.nn.sigmoid(a0_ref[...] + _dotx(xa, aup_ref[...]))
    g_ref[...] = _dotx(jax.nn.sigmoid(xg), gup_ref[...])
    kk = k * kk_ref_[...]
    kk = kk / jnp.maximum(jnp.sqrt(_dotx(kk * kk, g1_ref[...])), 1e-12)
    kk_ref[...] = kk
    kb_ref[...] = kk * a
    kp_ref[...] = k * (1.0 + (a - 1.0) * ka_ref[...])


def _rwkv_pre(shifted, w0, w_lora_up, a0, a_lora_up, g_lora_up, k_k, k_a, tm):
    t = shifted.shape[0]
    row = lambda w: pl.BlockSpec((tm, w), lambda i: (i, 0))
    const = lambda a: pl.BlockSpec(a.shape, lambda i: (0, 0))
    g1 = _head_ones()
    args = (shifted, w0, w_lora_up, a0, a_lora_up, g_lora_up, k_k, k_a, g1)
    out = jax.ShapeDtypeStruct((t, B_WIDTH), F32)
    return pl.pallas_call(
        _rwkv_pre_kernel, grid=(t // tm,),
        in_specs=[row(SHIFT_W)] + [const(a) for a in args[1:]],
        out_specs=(row(B_WIDTH),) * 5, out_shape=(out,) * 5,
        compiler_params=_cparams(("arbitrary",)), name="rwkv_prep",
    )(*args)


def _rwkv_seq_kernel(r_ref, v_ref, ld_ref, kk_ref, kb_ref, kp_ref, y_ref, so_ref, st_ref):
    c = CHUNK

    @pl.when(pl.program_id(0) == 0)
    def _():
        st_ref[...] = jnp.zeros(st_ref.shape, F32)

    ri = lax.broadcasted_iota(jnp.int32, (c, c), 0)
    ci = lax.broadcasted_iota(jnp.int32, (c, c), 1)
    incl, strict, eye = ri >= ci, ri > ci, ri == ci
    ld = ld_ref[...]
    tri = jnp.where(incl, 1.0, 0.0).astype(BF16)
    ld_hi = ld.astype(BF16)
    rest = ld - ld_hi.astype(F32)
    ld_mid = rest.astype(BF16)
    ld_lo = (rest - ld_mid.astype(F32)).astype(BF16)
    cum = _dot(tri, ld_hi) + (_dot(tri, ld_mid) + _dot(tri, ld_lo))
    cum_c = cum[c - 1:c, :]
    e_neg, e_rem = jnp.exp(-cum), jnp.exp(cum_c - cum)
    kk, kb, kp = kk_ref[...], kb_ref[...], kp_ref[...]
    a_all = -kk * jnp.exp(cum - ld)
    r_all = r_ref[...] * jnp.exp(cum)
    bt_all, kt_all = kb * e_neg, kp * e_neg
    bh_all, kh_all = kb * e_rem, kp * e_rem
    pc = jnp.exp(cum_c)
    v_all = v_ref[...]
    n = B_HEAD_DIM
    heads = range(B_HEADS)
    hsl = [slice(h * n, (h + 1) * n) for h in heads]
    p = [_dot3_t(jnp.concatenate([a_all[:, s], r_all[:, s]], 0), jnp.concatenate([bt_all[:, s], kt_all[:, s]], 0))
         for s in hsl]
    l_ak = [jnp.where(strict, q[:c, c:], 0.0) for q in p]
    m_rb = [jnp.where(incl, q[c:, :c], 0.0) for q in p]
    m_rk = [jnp.where(incl, q[c:, c:], 0.0) for q in p]
    lp = [jnp.where(strict, q[:c, :c], 0.0) for q in p]
    x = [jnp.concatenate([a_all[:, s], _dot3(l_ak[h], v_all[:, s])], 1) for h, s in enumerate(hsl)]
    n_dbl = int(np.log2(c))
    for it in range(n_dbl):
        x = [x[h] + _dot3(lp[h], x[h]) for h in heads]
        if it + 1 < n_dbl:
            lp = [_dot3(lp[h], lp[h]) for h in heads]
    bm = [_dot3(jnp.concatenate([bh_all[:, s].T, m_rb[h]], 0), x[h]) for h, s in enumerate(hsl)]
    kv = [_dot3(jnp.concatenate([kh_all[:, s].T, m_rk[h]], 0), v_all[:, s]) for h, s in enumerate(hsl)]
    for h, s in enumerate(hsl):
        m_c = jnp.where(eye, jnp.broadcast_to(pc[:, s], (n, n)), 0.0) + bm[h][:n, :n]
        g_c = r_all[:, s] + bm[h][n:, :n]
        gs = _dot3(jnp.concatenate([g_c, m_c], 0), st_ref[h])
        y_ref[:, s] = gs[:c, :] + (bm[h][n:, n:] + kv[h][n:, :])
        st_ref[h] = gs[c:, :] + (bm[h][:n, n:] + kv[h][:n, :])
    so_ref[...] = st_ref[...]


def _rwkv_seq(shifted, ld, kk, kb, kp):
    t = shifted.shape[0]
    assert t % CHUNK == 0
    col = lambda j: pl.BlockSpec((CHUNK, B_WIDTH), lambda i: (i, j))
    state = jax.ShapeDtypeStruct((B_HEADS, B_HEAD_DIM, B_HEAD_DIM), F32)
    return pl.pallas_call(
        _rwkv_seq_kernel, grid=(t // CHUNK,),
        in_specs=[col(0), col(2), col(0), col(0), col(0), col(0)],
        out_specs=(col(0), pl.BlockSpec(state.shape, lambda i: (0, 0, 0))),
        out_shape=(jax.ShapeDtypeStruct((t, B_WIDTH), F32), state),
        scratch_shapes=[pltpu.VMEM(state.shape, F32)],
        compiler_params=_cparams(("arbitrary",)), name="rwkv_chunked_scan",
    )(shifted, shifted, ld, kk, kb, kp)


RWKV_STEP_ROWS = 8


def _rwkv_step_kernel(s_ref, r_ref, ld_ref, kk_ref, kb_ref, kp_ref, v_ref, so_ref, y_ref):
    def one(b, carry):
        row = pl.ds(b, 1)
        r, w, kk, kb, kp = r_ref[row, :], jnp.exp(ld_ref[row, :]), kk_ref[row, :], kb_ref[row, :], kp_ref[row, :]
        for h in range(B_HEADS):
            hs = slice(h * B_HEAD_DIM, (h + 1) * B_HEAD_DIM)
            s = s_ref[b, h]
            sa = -jnp.sum(s * kk[:, hs], axis=-1, keepdims=True)
            s2 = s * w[:, hs] + sa * kb[:, hs] + v_ref[b, h] * kp[:, hs]
            so_ref[b, h] = s2
            y_ref[b, h] = jnp.sum(s2 * r[:, hs], axis=-1, keepdims=True)
        return carry
    lax.fori_loop(0, s_ref.shape[0], one, 0)


def _rwkv_step(state, shifted, ld, kk, kb, kp, v_col):
    bsz = state.shape[0]
    nb = RWKV_STEP_ROWS
    assert bsz % nb == 0
    row = lambda: pl.BlockSpec((nb, B_WIDTH), lambda i: (i, 0))
    st = pl.BlockSpec((nb, B_HEADS, B_HEAD_DIM, B_HEAD_DIM), lambda i: (i, 0, 0, 0))
    colv = pl.BlockSpec((nb, B_HEADS, B_HEAD_DIM, 1), lambda i: (i, 0, 0, 0))
    return pl.pallas_call(
        _rwkv_step_kernel, grid=(bsz // nb,),
        in_specs=[st, row(), row(), row(), row(), row(), colv],
        out_specs=(st, colv),
        out_shape=(jax.ShapeDtypeStruct(state.shape, F32),
                   jax.ShapeDtypeStruct((bsz, B_HEADS, B_HEAD_DIM, 1), F32)),
        compiler_params=_cparams(("arbitrary",)), name="rwkv_single_step",
    )(state, shifted, ld, kk, kb, kp, v_col)


def _rwkv_post_kernel(y_ref, r_ref, v_ref, kp_ref, g_ref, rk_ref, gng_ref, gnb_ref, g1_ref, o_ref):
    g1 = g1_ref[...]
    inv_n = 1.0 / B_HEAD_DIM
    y = y_ref[...]
    d = y - _dotx(y, g1) * inv_n
    var = _dotx(d * d, g1) * inv_n
    yn = d * lax.rsqrt(var + GN_EPS) * gng_ref[...] + gnb_ref[...]
    bonus = _dotx(r_ref[...] * kp_ref[...] * rk_ref[...], g1) * v_ref[...]
    o_ref[...] = (yn + bonus) * g_ref[...]


def _rwkv_post(y, shifted, kp, g, r_k, gn_g, gn_b, tm):
    t = y.shape[0]
    col = lambda j: pl.BlockSpec((tm, B_WIDTH), lambda i: (i, j))
    const = lambda a: pl.BlockSpec(a.shape, lambda i: (0, 0))
    g1 = _head_ones()
    return pl.pallas_call(
        _rwkv_post_kernel, grid=(t // tm,),
        in_specs=[col(0), col(0), col(2), col(0), col(0), const(r_k), const(gn_g), const(gn_b), const(g1)],
        out_specs=col(0), out_shape=jax.ShapeDtypeStruct((t, B_WIDTH), F32),
        compiler_params=_cparams(("arbitrary",)), name="rwkv_output_norm",
    )(y, shifted, shifted, kp, g, r_k, gn_g, gn_b, g1)


def _outproj_kernel(x_ref, attn_ref, rw_ref, gate_ref, wa_ref, wb_ref, wo_ref, nf_ref, wr_ref, br_ref,
                    h_ref, hn_ref, comb_ref):
    ga, gb = gate_ref[:, :D_MODEL], gate_ref[:, D_MODEL:]
    merged = ga * _dot(attn_ref[...].astype(BF16), wa_ref[...]) + gb * _dot(rw_ref[...].astype(BF16), wb_ref[...])
    h = x_ref[...] + _dot(merged.astype(BF16), wo_ref[...])
    h_ref[...] = h
    hn = _rms(h, nf_ref[...])
    hn_ref[...] = hn.astype(BF16)
    logits = _dotx(hn, wr_ref[...]) + br_ref[...]
    lane = lax.broadcasted_iota(jnp.int32, logits.shape, 1)
    big = jnp.int32(LANES)
    first = lambda hit: jnp.min(jnp.where(hit, lane, big), axis=-1, keepdims=True)
    is_g = lane < N_GROUPS
    gl = jnp.where(is_g, logits, -jnp.inf)
    gmax = jnp.max(gl, axis=-1, keepdims=True)
    p_group = 1.0 / jnp.sum(jnp.exp(gl - gmax), axis=-1, keepdims=True)
    g_sel = first(gl == gmax)
    e0 = N_GROUPS + g_sel * EXPERTS_PER_GROUP
    el = jnp.where(jnp.logical_and(lane >= e0, lane < e0 + EXPERTS_PER_GROUP), logits, -jnp.inf)
    v1 = jnp.max(el, axis=-1, keepdims=True)
    i1 = first(el == v1)
    el2 = jnp.where(lane == i1, -jnp.inf, el)
    v2 = jnp.max(el2, axis=-1, keepdims=True)
    i2 = first(el2 == v2)
    e21 = jnp.exp(v2 - v1)
    w1 = p_group / (1.0 + e21)
    w2 = p_group * e21 / (1.0 + e21)
    comb_ref[...] = jnp.where(lane == i1 - N_GROUPS, w1, 0.0) + jnp.where(lane == i2 - N_GROUPS, w2, 0.0)


def _outproj(x, attn, rw, gates, wa, wb, wo, norm_ffn, w_router, b_router, tm):
    t = x.shape[0]
    row = lambda w: pl.BlockSpec((tm, w), lambda i: (i, 0))
    const = lambda a: pl.BlockSpec(a.shape, lambda i: (0, 0))
    return pl.pallas_call(
        _outproj_kernel, grid=(t // tm,),
        in_specs=[row(D_MODEL), row(A_WIDTH), row(B_WIDTH), row(2 * D_MODEL), const(wa), const(wb), const(wo),
                  const(norm_ffn), const(w_router), const(b_router)],
        out_specs=(row(D_MODEL), row(D_MODEL), row(LANES)),
        out_shape=(jax.ShapeDtypeStruct((t, D_MODEL), F32), jax.ShapeDtypeStruct((t, D_MODEL), BF16),
                   jax.ShapeDtypeStruct((t, LANES), F32)),
        compiler_params=_cparams(("arbitrary",)), name="outproj_router",
    )(x, attn, rw, gates, wa, wb, wo, norm_ffn, w_router, b_router)


def _moe_kernel(h_ref, hn_ref, comb_ref, wg_ref, wu_ref, wd_ref, nfin_ref, o_ref, acc_ref):
    e = pl.program_id(1)
    hn = hn_ref[...]
    act = jax.nn.silu(_dot(hn, wg_ref[0])) * _dot(hn, wu_ref[0])
    y = _dot(act.astype(BF16), wd_ref[0])
    lane = lax.broadcasted_iota(jnp.int32, comb_ref.shape, 1)
    c_e = jnp.sum(jnp.where(lane == e, comb_ref[...], 0.0), axis=-1, keepdims=True)

    @pl.when(e == 0)
    def _():
        acc_ref[...] = c_e * y

    @pl.when(e > 0)
    def _():
        acc_ref[...] += c_e * y

    @pl.when(e == pl.num_programs(1) - 1)
    def _():
        o_ref[...] = _rms(h_ref[...] + acc_ref[...], nfin_ref[...])


def _moe(h, hn, comb, wg, wu, wd, norm_final, tm):
    t = h.shape[0]
    row = lambda w: pl.BlockSpec((tm, w), lambda i, e: (i, 0))
    return pl.pallas_call(
        _moe_kernel, grid=(t // tm, N_EXPERTS),
        in_specs=[row(D_MODEL), row(D_MODEL), row(LANES),
                  pl.BlockSpec((1, D_MODEL, D_EXPERT), lambda i, e: (e, 0, 0)),
                  pl.BlockSpec((1, D_MODEL, D_EXPERT), lambda i, e: (e, 0, 0)),
                  pl.BlockSpec((1, D_EXPERT, D_MODEL), lambda i, e: (e, 0, 0)),
                  pl.BlockSpec(norm_final.shape, lambda i, e: (0, 0))],
        out_specs=row(D_MODEL), out_shape=jax.ShapeDtypeStruct((t, D_MODEL), F32),
        scratch_shapes=[pltpu.VMEM((tm, D_MODEL), F32)],
        compiler_params=_cparams(("arbitrary", "arbitrary")), name="moe_experts",
    )(h, hn, comb, wg, wu, wd, norm_final)


def _tile(t, pref):
    return min(pref, t)


def _mixer_tail(x, attn, shifted, gates, state_step, weights, ffn):
    (w0, w_lora_up, a0, a_lora_up, g_lora_up, k_k, k_a, r_k, gn_g, gn_b, wa, wb, wo) = weights
    (norm_ffn, w_router, b_router, wg, wu, wd, norm_final) = ffn
    t = x.shape[0]
    tm = _tile(t, 256)
    ld, kk, kb, kp, g = _rwkv_pre(shifted, w0, w_lora_up, a0, a_lora_up, g_lora_up, k_k, k_a, tm)
    if state_step is None:
        y, st = _rwkv_seq(shifted, ld, kk, kb, kp)
        wkv = jnp.swapaxes(st, 1, 2)[None]
    else:
        v_col = shifted[:, 2 * B_WIDTH:3 * B_WIDTH].reshape(t, B_HEADS, B_HEAD_DIM, 1)
        wkv, y_col = _rwkv_step(state_step, shifted, ld, kk, kb, kp, v_col)
        y = y_col.reshape(t, B_WIDTH)
    rw = _rwkv_post(y, shifted, kp, g, r_k, gn_g, gn_b, tm)
    h, hn, comb = _outproj(x, attn, rw, gates, wa, wb, wo, norm_ffn, w_router, b_router, tm)
    out = _moe(h, hn, comb, wg, wu, wd, norm_final, _tile(t, 1024))
    return out, wkv


def kernel(x_prompt, x_sample, cache_k, cache_v, cache_idx_k, state_wkv, state_shift, page_table, norm_mix, w_in, mu_shift, w0, w_lora_up, a0, a_lora_up, g_lora_up, k_k, k_a, r_k, gn_g, gn_b, w_branch_a, w_branch_b, w_out, norm_ffn, w_router_group, b_router_group, w_router_expert, b_router_expert, w_gate, w_up, w_down, norm_final):
    assert w_in.shape[0] == 1, "single-layer kernel"
    bp, tp, _ = x_prompt.shape
    bs, ts, _ = x_sample.shape
    assert bp == 1 and ts == 1
    n_pages = page_table.shape[1]
    past = n_pages * PAGE_SIZE
    row2 = lambda a: a.reshape(1, -1)

    w_packed = _pack_w_in(w_in[0])
    mix_w = (row2(w0[0]), w_lora_up[0], row2(a0[0]), a_lora_up[0], g_lora_up[0], row2(k_k[0]), row2(k_a[0]),
             row2(r_k[0]), row2(gn_g[0]), row2(gn_b[0]),
             w_branch_a[0].astype(BF16), w_branch_b[0].astype(BF16), w_out[0].astype(BF16))
    pad_r = jnp.zeros((D_MODEL, LANES - N_GROUPS - N_EXPERTS), F32)
    w_router = jnp.concatenate([w_router_group[0], w_router_expert[0], pad_r], 1)
    b_router = jnp.concatenate([b_router_group[0], b_router_expert[0], pad_r[0]])[None]
    ffn_w = (row2(norm_ffn[0]), w_router, b_router, w_gate[0].astype(BF16), w_up[0].astype(BF16),
             w_down[0].astype(BF16), row2(norm_final))
    g_mix, mu = row2(norm_mix[0]), row2(mu_shift[0])

    xp = x_prompt[0]
    tabs = _rope_tables(jnp.arange(tp))
    (q, kf, kb16, vf, vb16, qi, kiwi, ki16, shifted, shift_last, gates) = _inproj(
        xp, g_mix, w_packed, *tabs, mu, jnp.zeros((1, SHIFT_W), F32), True, _tile(tp, 256))
    attn = _dsa_prompt(q, kb16, vb16, qi, kiwi, ki16)
    y_p, wkv_p = _mixer_tail(xp, attn, shifted, gates, None, mix_w, ffn_w)

    xs = x_sample[:, 0]
    tabs_s = _rope_tables(jnp.full((bs,), past, jnp.int32))
    (q_s, kf_s, _, vf_s, _, qi_s, kiwi_s, ki16_s, shifted_s, shift_raw_s, gates_s) = _inproj(
        xs, g_mix, w_packed, *tabs_s, mu, state_shift[0], False, _tile(bs, 256))
    pad_h = IDX_ROWS - IDX_HEADS
    qi3 = jnp.pad(qi_s.reshape(bs, IDX_HEADS, IDX_DIM), ((0, 0), (0, pad_h), (0, 0)))
    wi3 = jnp.pad(kiwi_s[:, IDX_DIM:IDX_DIM + IDX_HEADS], ((0, 0), (0, pad_h)))[:, :, None]
    scores = _dsa_sample_scores(page_table, qi3, wi3, ki16_s[:, None, :], jnp.swapaxes(cache_idx_k, 2, 3))
    k_sel = min(TOPK_MAX, (past + ts) // 4)
    mask = _topk_mask(scores.reshape(bs, -1), k_sel).reshape(scores.shape)
    cols = lambda a: jnp.swapaxes(a.reshape(bs, A_HEADS, HEAD_DIM), 1, 2)
    page_t = lambda cache: jnp.transpose(cache, (0, 1, 3, 4, 2))
    attn_t = _dsa_sample_attn(page_table, cols(q_s), cols(kf_s), cols(vf_s), mask, page_t(cache_k), page_t(cache_v))
    attn_s = jnp.swapaxes(attn_t[:, :, :A_HEADS], 1, 2).reshape(bs, A_WIDTH)
    y_s, wkv_s = _mixer_tail(xs, attn_s, shifted_s, gates_s, state_wkv[0], mix_w, ffn_w)

    kv5 = lambda a, b, t: a.reshape(1, b, t, A_HEADS, HEAD_DIM)
    return (y_p[None], y_s[:, None],
            kv5(kf, 1, tp), kv5(vf, 1, tp), kiwi[:, :IDX_DIM].reshape(1, 1, tp, IDX_DIM),
            wkv_p[None], shift_last.reshape(1, 1, SHIFT_W),
            kv5(kf_s, bs, 1), kv5(vf_s, bs, 1), kiwi_s[:, :IDX_DIM].reshape(1, bs, 1, IDX_DIM),
            wkv_s[None], shift_raw_s[None])
```

```python
import functools

import jax
import jax.numpy as jnp
import numpy as np
from jax import lax
from jax.experimental import pallas as pl
from jax.experimental.pallas import tpu as pltpu

F32 = jnp.float32
BF16 = jnp.bfloat16

D_MODEL = 1024
PAGE_SIZE = 128
A_HEADS = 8
HEAD_DIM = 64
A_WIDTH = A_HEADS * HEAD_DIM
IDX_HEADS = 4
IDX_DIM = 64
TOPK_MAX = 256
ROT_DIM = HEAD_DIM // 4
ROPE_THETA = 500000.0
B_HEADS = 8
B_HEAD_DIM = 64
B_WIDTH = B_HEADS * B_HEAD_DIM
DECAY_LORA = 64
AAA_LORA = 64
GATE_LORA = 128
GN_EPS = B_HEAD_DIM * 1e-5
SHIFT_W = 3 * B_WIDTH + DECAY_LORA + AAA_LORA + GATE_LORA
N_GROUPS = 4
EXPERTS_PER_GROUP = 8
N_EXPERTS = N_GROUPS * EXPERTS_PER_GROUP
D_EXPERT = 512
RMS_EPS = 1e-6

LANES = 128
SUBLANES = 8
VMEM_LIMIT_BYTES = 56 * 1024 * 1024

QKV_END = 3 * A_WIDTH
QI_END = QKV_END + IDX_HEADS * IDX_DIM
KIWI_END = QI_END + LANES
SHIFT_END = KIWI_END + SHIFT_W
D_IN_PACKED = SHIFT_END + 2 * D_MODEL
IDX_W_SCALE = IDX_HEADS ** -0.5 * IDX_DIM ** -0.5
NEG_BIG = -1e30
CHUNK = 64


def _cparams(sem):
    return pltpu.CompilerParams(dimension_semantics=sem, vmem_limit_bytes=VMEM_LIMIT_BYTES)


def _dot(a, b):
    return jnp.dot(a, b, preferred_element_type=F32)


def _dot_t(a, b):
    return lax.dot_general(a, b, (((1,), (1,)), ((), ())), preferred_element_type=F32)


def _dotx(a, b):
    return jnp.dot(a, b, preferred_element_type=F32, precision=lax.Precision.HIGHEST)


def _dotx_t(a, b):
    return lax.dot_general(a, b, (((1,), (1,)), ((), ())), preferred_element_type=F32,
                           precision=lax.Precision.HIGHEST)


def _split2(a):
    hi = a.astype(BF16)
    return hi, (a - hi.astype(F32)).astype(BF16)


def _dot3(a, b):
    ah, al = _split2(a)
    bh, bl = _split2(b)
    return _dot(ah, bh) + (_dot(ah, bl) + _dot(al, bh))


def _dot3_t(a, b):
    ah, al = _split2(a)
    bh, bl = _split2(b)
    return _dot_t(ah, bh) + (_dot_t(ah, bl) + _dot_t(al, bh))


def _rms(x, g):
    return x * lax.rsqrt(jnp.mean(x * x, axis=-1, keepdims=True) + RMS_EPS) * g


def _rope_slab(x, c, sa, sb):
    return x * c + pltpu.roll(x, ROT_DIM // 2, 1) * sa + pltpu.roll(x, LANES - ROT_DIM // 2, 1) * sb


def _inproj_kernel(seq_mode, x_ref, g_ref, w_ref, c_ref, sa_ref, sb_ref, mu_ref, prev_ref,
                   q_ref, kf_ref, kb_ref, vf_ref, vb_ref, qi_ref, kiwi_ref, kib_ref, sh_ref, last_ref,
                   gate_ref, carry_ref):
    tm = x_ref.shape[0]
    xn = _rms(x_ref[...], g_ref[...]).astype(BF16)
    c, sa, sb = c_ref[...], sa_ref[...], sb_ref[...]

    def proj(c0, c1):
        return _dot(xn, w_ref[:, c0:c1])

    def rope(z):
        return jnp.concatenate(
            [_rope_slab(z[:, s:s + LANES], c, sa, sb) for s in range(0, z.shape[1], LANES)], axis=1)

    q = rope(proj(0, A_WIDTH)) * (HEAD_DIM ** -0.5)
    q_ref[...] = q.astype(BF16)
    k = rope(proj(A_WIDTH, 2 * A_WIDTH))
    kf_ref[...] = k
    kb_ref[...] = k.astype(BF16)
    v = proj(2 * A_WIDTH, QKV_END)
    vf_ref[...] = v
    vb_ref[...] = v.astype(BF16)
    qi_ref[...] = rope(proj(QKV_END, QI_END)).astype(BF16)
    kiwi = proj(QI_END, KIWI_END)
    lane = lax.broadcasted_iota(jnp.int32, kiwi.shape, 1)
    kiwi = jnp.where(lane < IDX_DIM, _rope_slab(kiwi, c, sa, sb), kiwi * IDX_W_SCALE)
    kiwi_ref[...] = kiwi
    kib_ref[...] = kiwi[:, :IDX_DIM].astype(BF16)

    u = proj(KIWI_END, SHIFT_END)
    if seq_mode:
        @pl.when(pl.program_id(0) == 0)
        def _():
            carry_ref[...] = prev_ref[...]
        row = lax.broadcasted_iota(jnp.int32, u.shape, 0)
        u_prev = jnp.where(row == 0, carry_ref[...], pltpu.roll(u, 1, 0))
        carry_ref[...] = u[tm - 1:tm, :]
    else:
        u_prev = prev_ref[...]
    sh_ref[...] = u + (u_prev - u) * mu_ref[...]
    last_ref[...] = u[tm - 1:tm, :] if seq_mode else u
    gate_ref[...] = jax.nn.sigmoid(proj(SHIFT_END, D_IN_PACKED))


def _inproj(x, norm_g, w_packed, rope_c, rope_sa, rope_sb, mu, prev, seq_mode, tm):
    t = x.shape[0]
    assert t % tm == 0
    row = lambda w: pl.BlockSpec((tm, w), lambda i: (i, 0))
    const = lambda a: pl.BlockSpec(a.shape, lambda i: (0, 0))
    prev_spec = const(prev) if seq_mode else row(SHIFT_W)
    out_shape = (
        jax.ShapeDtypeStruct((t, A_WIDTH), BF16),
        jax.ShapeDtypeStruct((t, A_WIDTH), F32), jax.ShapeDtypeStruct((t, A_WIDTH), BF16),
        jax.ShapeDtypeStruct((t, A_WIDTH), F32), jax.ShapeDtypeStruct((t, A_WIDTH), BF16),
        jax.ShapeDtypeStruct((t, IDX_HEADS * IDX_DIM), BF16),
        jax.ShapeDtypeStruct((t, LANES), F32),
        jax.ShapeDtypeStruct((t, IDX_DIM), BF16),
        jax.ShapeDtypeStruct((t, SHIFT_W), F32),
        jax.ShapeDtypeStruct((1 if seq_mode else t, SHIFT_W), F32),
        jax.ShapeDtypeStruct((t, 2 * D_MODEL), F32),
    )
    out_specs = (row(A_WIDTH), row(A_WIDTH), row(A_WIDTH), row(A_WIDTH), row(A_WIDTH),
                 row(IDX_HEADS * IDX_DIM), row(LANES), row(IDX_DIM), row(SHIFT_W),
                 pl.BlockSpec((1, SHIFT_W), lambda i: (0, 0)) if seq_mode else row(SHIFT_W), row(2 * D_MODEL))
    return pl.pallas_call(
        functools.partial(_inproj_kernel, seq_mode),
        grid=(t // tm,),
        in_specs=[row(D_MODEL), const(norm_g), const(w_packed), row(LANES), row(LANES), row(LANES),
                  const(mu), prev_spec],
        out_specs=out_specs, out_shape=out_shape,
        scratch_shapes=[pltpu.VMEM((1, SHIFT_W), F32)],
        compiler_params=_cparams(("arbitrary",)),
        name="inproj_seq" if seq_mode else "inproj_batch",
    )(x, norm_g, w_packed, rope_c, rope_sa, rope_sb, mu, prev)


def _rope_tables(pos):
    half = ROT_DIM // 2
    inv = ROPE_THETA ** (-jnp.arange(half, dtype=F32) / half)
    ang = pos.astype(F32)[:, None] * inv[None, :]
    cos, sin = jnp.cos(ang), jnp.sin(ang)
    t = pos.shape[0]
    pad = jnp.zeros((t, HEAD_DIM - ROT_DIM), F32)
    zero = jnp.zeros((t, half), F32)
    c = jnp.concatenate([cos, cos, pad + 1.0], 1)
    sa = jnp.concatenate([zero, sin, pad], 1)
    sb = jnp.concatenate([-sin, zero, pad], 1)
    rep = lambda a: jnp.concatenate([a] * (LANES // HEAD_DIM), 1)
    return rep(c), rep(sa), rep(sb)


def _pack_w_in(w_in):
    lead = QI_END + IDX_DIM + IDX_HEADS
    pad = jnp.zeros((D_MODEL, KIWI_END - lead), w_in.dtype)
    return jnp.concatenate([w_in[:, :lead], pad, w_in[:, lead:]], 1).astype(BF16)


DSA_TQ = 256
DSA_TK = 512
THR_ROWS = 128
THR_WARM_STEPS = 22
THR_SLOT_TARGET = 12
THR_CHEAP_STEPS = 1
THR_UNROLL = DSA_TK // LANES
THR_SLOTS = 4
THR_HALF = 64


def _causal_pairs(t, tq, tk):
    qi, kj, last = [], [], []
    for i in range(t // tq):
        nk = (i * tq + tq - 1) // tk + 1
        for j in range(nk):
            qi.append(i), kj.append(j), last.append(int(j == nk - 1))
    return (jnp.asarray(qi, jnp.int32), jnp.asarray(kj, jnp.int32), jnp.asarray(last, jnp.int32))


def _lane_tile(x, width):
    return jnp.concatenate([x] * (width // LANES), axis=1)


def _index_scores(qi, ki, wi, q0, k0):
    tq, tk = qi.shape[0], ki.shape[0]
    s = None
    for h in range(IDX_HEADS):
        sh = jnp.maximum(_dot_t(qi[:, h * IDX_DIM:(h + 1) * IDX_DIM], ki), 0.0)
        sh = sh * _lane_tile(wi[:, h * LANES:(h + 1) * LANES], tk)
        s = sh if s is None else s + sh
    qpos = q0 + lax.broadcasted_iota(jnp.int32, (tq, tk), 0)
    kpos = k0 + lax.broadcasted_iota(jnp.int32, (tq, tk), 1)
    return jnp.where(kpos <= qpos, s, -jnp.inf)


def _kth_largest_rows(sc_ref, r0, ncol, kf):
    n_rows = kf.shape[0]
    half = min(THR_HALF, n_rows)
    rows = pl.ds(r0, n_rows)
    shape = (n_rows, LANES)

    def col_reduce(fn, init, combine):
        def body(c, acc):
            for u in range(THR_UNROLL):
                acc = combine(acc, fn(sc_ref[c * THR_UNROLL + u, rows, :], 1.0))
            return acc
        return lax.fori_loop(0, ncol // THR_UNROLL, body, jnp.full(shape, init, F32))

    def count_where(reduce, pred):
        return jnp.sum(reduce(lambda b, w: jnp.where(pred(b), w, 0.0), 0.0, jnp.add), axis=-1, keepdims=True)

    def count_gt(reduce, m):
        mb = jnp.broadcast_to(m, shape)
        return count_where(reduce, lambda b: b > mb)

    def snap(reduce, lx, hi):
        lb, hb = jnp.broadcast_to(lx, shape), jnp.broadcast_to(hi, shape)
        lo_acc = reduce(lambda b, w: jnp.where(b > lb, b, jnp.inf), jnp.inf, jnp.minimum)
        hi_acc = reduce(lambda b, w: jnp.where(b <= hb, b, -jnp.inf), -jnp.inf, jnp.maximum)
        return jnp.min(lo_acc, axis=-1, keepdims=True), jnp.max(hi_acc, axis=-1, keepdims=True)

    def bisect(reduce, k, bounds):
        lx, hi, a, b, c_lx, c_hi = bounds
        m = 0.5 * a + 0.5 * b
        c = count_gt(reduce, m)
        up = c >= k
        return (jnp.where(up, m, lx), jnp.where(up, hi, m), jnp.where(up, m, a), jnp.where(up, b, m),
                jnp.where(up, c, c_lx), jnp.where(up, c_hi, c))

    def search(reduce, k, lx, hi):
        def round_(state):
            lx, hi, _, _ = state
            d_lo, d_hi = snap(reduce, lx, hi)
            m = 0.5 * d_lo + 0.5 * d_hi
            m = jnp.where(m < d_hi, m, d_lo)
            up = count_gt(reduce, m) >= k
            lx, hi = jnp.where(up, m, lx), jnp.where(up, hi, m)
            bounds = (lx, hi, jnp.maximum(lx, d_lo), jnp.minimum(hi, d_hi), jnp.zeros_like(lx), jnp.zeros_like(lx))
            for _ in range(THR_CHEAP_STEPS):
                bounds = bisect(reduce, k, bounds)
            open_rows = jnp.max(jnp.where(d_lo < d_hi, 1.0, 0.0))
            return bounds[0], bounds[1], d_lo, open_rows
        init = (lx, hi, jnp.zeros((n_rows, 1), F32), jnp.float32(1.0))
        return lax.while_loop(lambda s: s[3] > 0.0, round_, init)[2]

    lx = jnp.full((n_rows, 1), -jnp.inf, F32)
    hi = jnp.full((n_rows, 1), jnp.inf, F32)
    d_lo, d_hi = snap(col_reduce, lx, hi)
    n_finite = count_gt(col_reduce, lx)
    n_zero = count_where(col_reduce, lambda b: b == 0.0)
    zeros_inside = lambda lo, up: jnp.where(jnp.logical_and(lo < 0.0, up >= 0.0), n_zero, 0.0)

    def warm(state):
        bounds = bisect(col_reduce, kf, state[:6])
        held = bounds[4] - bounds[5] - zeros_inside(bounds[0], bounds[1])
        crowded = jnp.max(held) > float(THR_SLOT_TARGET)
        return (*bounds, state[6] + 1, jnp.logical_and(crowded, state[6] + 1 < THR_WARM_STEPS))
    warm_init = (lx, hi, d_lo, d_hi, n_finite, jnp.zeros((n_rows, 1), F32), jnp.int32(0), jnp.bool_(True))
    lx, hi, _, _, _, c_hi, _, _ = lax.while_loop(lambda st: st[7], warm, warm_init)

    def compact(h0):
        hrows = pl.ds(r0 + h0, half)
        hshape = (half, LANES)
        lb = jnp.broadcast_to(lx[h0:h0 + half], hshape)
        hb = jnp.broadcast_to(hi[h0:h0 + half], hshape)

        def body(c, carry):
            cnt, bufs = carry[0], list(carry[1:])
            for u in range(THR_UNROLL):
                v = sc_ref[c * THR_UNROLL + u, hrows, :]
                inside = jnp.where(v == 0.0, jnp.inf, jnp.where(v > lb, v, jnp.inf)) <= hb
                key = jnp.where(inside, cnt, -1.0)
                bufs = [jnp.where(key == float(slot), v, buf) for slot, buf in enumerate(bufs)]
                cnt = cnt + jnp.where(inside, 1.0, 0.0)
            return (cnt, *bufs)
        empty = jnp.full(hshape, -jnp.inf, F32)
        return lax.fori_loop(0, ncol // THR_UNROLL, body, (jnp.zeros(hshape, F32),) + (empty,) * THR_SLOTS)

    halves = [compact(h0) for h0 in range(0, n_rows, half)]
    cnt = jnp.concatenate([h[0] for h in halves], axis=0)
    bufs = [jnp.concatenate([h[1 + slot] for h in halves], axis=0) for slot in range(THR_SLOTS)]

    z_in = zeros_inside(lx, hi)
    lane0 = lax.broadcasted_iota(jnp.int32, shape, 1) == 0
    zero_tile = jnp.where(jnp.logical_and(lane0, z_in > 0.0), 0.0, -jnp.inf)
    zero_mult = jnp.where(lane0, z_in, 0.0)

    def small_reduce(fn, init, combine):
        acc = combine(jnp.full(shape, init, F32), fn(zero_tile, zero_mult))
        for buf in bufs:
            acc = combine(acc, fn(buf, 1.0))
        return acc

    def from_slots():
        thr = search(small_reduce, kf - c_hi, lx, hi)
        tb = jnp.broadcast_to(thr, shape)
        return (thr, c_hi + count_where(small_reduce, lambda b: b > tb),
                c_hi + count_where(small_reduce, lambda b: b >= tb))

    def from_all():
        thr = search(col_reduce, kf, lx, hi)
        tb = jnp.broadcast_to(thr, shape)
        return thr, count_where(col_reduce, lambda b: b > tb), count_where(col_reduce, lambda b: b >= tb)

    return lax.cond(jnp.max(cnt) > float(THR_SLOTS), from_all, from_slots)


def _tie_cutoff_rows(sc_ref, r0, ncol, thr, need):
    n_rows = thr.shape[0]
    rows = pl.ds(r0, n_rows)
    shape = (n_rows, LANES)
    tb, nb = jnp.broadcast_to(thr, shape), jnp.broadcast_to(need, shape)
    lane = lax.broadcasted_iota(jnp.int32, shape, 1).astype(F32)
    ri = lax.broadcasted_iota(jnp.int32, (LANES, LANES), 0)
    ci = lax.broadcasted_iota(jnp.int32, (LANES, LANES), 1)
    prefix = jnp.where(ri <= ci, 1.0, 0.0).astype(BF16)
    total = jnp.ones((LANES, LANES), BF16)

    def body(c, carry):
        run, best = carry
        eq = sc_ref[c, rows, :] == tb
        hits = jnp.where(eq, 1.0, 0.0).astype(BF16)
        found = jnp.logical_and(eq, run + _dot(hits, prefix) == nb)
        best = jnp.where(found, lane + (c * LANES).astype(F32), best)
        return run + _dot(hits, total), best
    _, best = lax.fori_loop(0, ncol, body, (jnp.zeros(shape, F32), jnp.full(shape, -1.0, F32)))
    return jnp.max(best, axis=-1, keepdims=True).astype(jnp.int32)


def _dsa_thr_kernel(qi_blk, kj_blk, last_blk, qi_ref, wi_ref, ki_ref, thr_ref, cut_ref, sc_ref):
    p = pl.program_id(0)
    i, j = qi_blk[p], kj_blk[p]
    tq, tk = qi_ref.shape[0], ki_ref.shape[0]
    s = _index_scores(qi_ref[...], ki_ref[...], wi_ref[...], i * tq, j * tk)
    for u in range(tk // LANES):
        sc_ref[j * (tk // LANES) + u] = s[:, u * LANES:(u + 1) * LANES]
    n_keys = sc_ref.shape[0] * LANES

    @pl.when(last_blk[p] == 1)
    def _():
        ncol = (j + 1) * (tk // LANES)
        for g in range(tq // THR_ROWS):
            r0 = g * THR_ROWS
            qpos = i * tq + r0 + lax.broadcasted_iota(jnp.int32, (THR_ROWS, 1), 0)
            kf = jnp.minimum(qpos + 1, TOPK_MAX).astype(F32)
            thr, n_gt, n_ge = _kth_largest_rows(sc_ref, r0, ncol, kf)
            thr_ref[r0:r0 + THR_ROWS, :] = jnp.broadcast_to(thr, (THR_ROWS, LANES))
            cut_ref[r0:r0 + THR_ROWS, :] = jnp.full((THR_ROWS, LANES), n_keys, jnp.int32)

            @pl.when(jnp.max(n_ge - kf) > 0.0)
            def _():
                cut = _tie_cutoff_rows(sc_ref, r0, ncol, thr, kf - n_gt)
                cut = jnp.where(n_ge > kf, cut, n_keys)
                cut_ref[r0:r0 + THR_ROWS, :] = jnp.broadcast_to(cut, (THR_ROWS, LANES))


ATTN_ROWS = 64


def _dsa_attn_kernel(qi_blk, kj_blk, last_blk, q_ref, qi_ref, wi_ref, thr_ref, cut_ref, ki_ref, k_ref, vx_ref,
                     o_ref, m_ref, acc_ref, bias_ref, s_ref, p_ref, alpha_ref):
    p = pl.program_id(0)
    i, j = qi_blk[p], kj_blk[p]
    tq, tk = q_ref.shape[1], k_ref.shape[1]

    @pl.when(j == 0)
    def _():
        m_ref[...] = jnp.full(m_ref.shape, NEG_BIG, F32)
        acc_ref[...] = jnp.zeros(acc_ref.shape, F32)

    s = _index_scores(qi_ref[...], ki_ref[...], wi_ref[...], i * tq, j * tk)
    thr = _lane_tile(thr_ref[...], tk)
    kpos = j * tk + lax.broadcasted_iota(jnp.int32, (tq, tk), 1)
    sel = jnp.logical_or(s > thr, jnp.logical_and(s == thr, kpos <= _lane_tile(cut_ref[...], tk)))
    bias_ref[...] = jnp.where(sel, 0.0, NEG_BIG)

    def logits(h):
        s_ref[h % 2] = _dot_t(q_ref[h], k_ref[h])

    def softmax(h):
        slot = h % 2
        for r0 in range(0, tq, ATTN_ROWS):
            rows = slice(r0, r0 + ATTN_ROWS)
            x = s_ref[slot, rows, :] + bias_ref[rows, :]
            m_prev = m_ref[h, rows, :]
            m_new = jnp.maximum(m_prev, jnp.max(x, axis=-1, keepdims=True))
            p_ref[slot, rows, :] = jnp.exp(x - _lane_tile(m_new, tk)).astype(BF16)
            alpha_ref[slot, rows, :] = jnp.exp(m_prev - m_new)
            m_ref[h, rows, :] = m_new

    def values(h):
        acc_ref[h] = alpha_ref[h % 2] * acc_ref[h] + _dot(p_ref[h % 2], vx_ref[h])

    logits(0)
    logits(1)
    softmax(0)

    def head_step(h, carry):
        values(h - 2)
        softmax(h - 1)
        logits(h)
        return carry
    lax.fori_loop(2, A_HEADS, head_step, 0)
    softmax(A_HEADS - 1)
    values(A_HEADS - 2)
    values(A_HEADS - 1)

    @pl.when(last_blk[p] == 1)
    def _():
        o_ref[...] = jnp.concatenate(
            [acc_ref[h][:, 0:HEAD_DIM] / acc_ref[h][:, HEAD_DIM:HEAD_DIM + 1] for h in range(A_HEADS)], axis=1)


def _dsa_prompt(q, k, v, qi, kiwi, ki):
    t = q.shape[0]
    tq, tk = min(DSA_TQ, t), min(DSA_TK, t)
    assert t % tq == 0 and t % tk == 0 and tq % THR_ROWS == 0
    kiwi = jnp.repeat(kiwi[:, IDX_DIM:IDX_DIM + IDX_HEADS], LANES, axis=1)
    pairs = _causal_pairs(t, tq, tk)
    n_pairs = pairs[0].shape[0]
    qrow = lambda w: pl.BlockSpec((tq, w), lambda p, qb, kb, lb: (qb[p], 0))
    krow = lambda w: pl.BlockSpec((tk, w), lambda p, qb, kb, lb: (kb[p], 0))
    thr, cut = pl.pallas_call(
        _dsa_thr_kernel,
        grid_spec=pltpu.PrefetchScalarGridSpec(
            num_scalar_prefetch=3, grid=(n_pairs,),
            in_specs=[qrow(IDX_HEADS * IDX_DIM), qrow(IDX_HEADS * LANES), krow(IDX_DIM)],
            out_specs=(qrow(LANES), qrow(LANES)),
            scratch_shapes=[pltpu.VMEM((t // LANES, tq, LANES), F32)]),
        out_shape=(jax.ShapeDtypeStruct((t, LANES), F32), jax.ShapeDtypeStruct((t, LANES), jnp.int32)),
        compiler_params=_cparams(("arbitrary",)), name="dsa_prompt_threshold",
    )(*pairs, qi, kiwi, ki)
    qhead = pl.BlockSpec((A_HEADS, tq, HEAD_DIM), lambda p, qb, kb, lb: (0, qb[p], 0))
    khead = lambda w: pl.BlockSpec((A_HEADS, tk, w), lambda p, qb, kb, lb: (0, kb[p], 0))
    heads = lambda a: jnp.swapaxes(a.reshape(t, A_HEADS, HEAD_DIM), 0, 1)
    return pl.pallas_call(
        _dsa_attn_kernel,
        grid_spec=pltpu.PrefetchScalarGridSpec(
            num_scalar_prefetch=3, grid=(n_pairs,),
            in_specs=[qhead, qrow(IDX_HEADS * IDX_DIM), qrow(IDX_HEADS * LANES), qrow(LANES), qrow(LANES),
                      krow(IDX_DIM), khead(HEAD_DIM), khead(LANES)],
            out_specs=qrow(A_WIDTH),
            scratch_shapes=[pltpu.VMEM((A_HEADS, tq, LANES), F32), pltpu.VMEM((A_HEADS, tq, LANES), F32),
                            pltpu.VMEM((tq, tk), F32), pltpu.VMEM((2, tq, tk), F32),
                            pltpu.VMEM((2, tq, tk), BF16), pltpu.VMEM((2, tq, LANES), F32)]),
        out_shape=jax.ShapeDtypeStruct((t, A_WIDTH), F32),
        compiler_params=_cparams(("arbitrary",)), name="dsa_prompt_attention",
    )(*pairs, heads(q), qi, kiwi, thr, cut, ki, heads(k), _values_with_ones(heads(v)))


def _values_with_ones(v3):
    ones = jnp.ones(v3.shape[:2] + (1,), v3.dtype)
    pad = jnp.zeros(v3.shape[:2] + (LANES - HEAD_DIM - 1,), v3.dtype)
    return jnp.concatenate([v3, ones, pad], axis=-1)


IDX_ROWS = 16


def _dsa_sample_scores_kernel(pt_ref, qi_ref, wi_ref, kin_ref, cache_ref, o_ref, buf_ref, sem_ref):
    b, nb = pl.program_id(0), pl.num_programs(0)
    n_pages = pt_ref.shape[1]
    past = n_pages * PAGE_SIZE

    def issue(bb, slot):
        for p in range(n_pages):
            pltpu.make_async_copy(cache_ref.at[0, pt_ref[bb, p]], buf_ref.at[slot, p], sem_ref.at[slot]).start()

    @pl.when(b == 0)
    def _():
        issue(0, 0)

    @pl.when(b + 1 < nb)
    def _():
        issue(b + 1, (b + 1) % 2)

    slot = b % 2
    pltpu.make_async_copy(buf_ref.at[slot], buf_ref.at[slot], sem_ref.at[slot]).wait()
    qi, wi = qi_ref[0], wi_ref[0]

    def pages(g, carry):
        rows = []
        for u in range(SUBLANES):
            s = jnp.maximum(_dot(qi, buf_ref[slot, g * SUBLANES + u].astype(BF16)), 0.0) * wi
            rows.append(jnp.sum(s, axis=0, keepdims=True))
        o_ref[0, pl.ds(pl.multiple_of(g * SUBLANES, SUBLANES), SUBLANES), :] = jnp.concatenate(rows, axis=0)
        return carry
    lax.fori_loop(0, n_pages // SUBLANES, pages, 0)
    kin = kin_ref[0].astype(F32)
    s_new = jnp.maximum(jnp.sum(qi.astype(F32) * kin, axis=-1, keepdims=True), 0.0) * wi
    s_new = jnp.sum(s_new, axis=0, keepdims=True)
    lane = lax.broadcasted_iota(jnp.int32, (SUBLANES, LANES), 1)
    row = lax.broadcasted_iota(jnp.int32, (SUBLANES, LANES), 0)
    o_ref[0, n_pages:n_pages + SUBLANES, :] = jnp.where(jnp.logical_and(lane == 0, row == 0), s_new, -jnp.inf)


def _dsa_sample_scores(page_table, qi3, wi3, ki_new, cache_idx_t):
    bsz, n_pages = page_table.shape
    assert n_pages % SUBLANES == 0
    n_rows = n_pages + SUBLANES
    blk = lambda a: pl.BlockSpec((1,) + a.shape[1:], lambda b, pt: (b, 0, 0))
    return pl.pallas_call(
        _dsa_sample_scores_kernel,
        grid_spec=pltpu.PrefetchScalarGridSpec(
            num_scalar_prefetch=1, grid=(bsz,),
            in_specs=[blk(qi3), blk(wi3), blk(ki_new), pl.BlockSpec(memory_space=pl.ANY)],
            out_specs=pl.BlockSpec((1, n_rows, LANES), lambda b, pt: (b, 0, 0)),
            scratch_shapes=[pltpu.VMEM((2, n_pages, IDX_DIM, PAGE_SIZE), F32), pltpu.SemaphoreType.DMA((2,))]),
        out_shape=jax.ShapeDtypeStruct((bsz, n_rows, LANES), F32),
        compiler_params=_cparams(("arbitrary",)), name="dsa_sample_scores",
    )(page_table, qi3, wi3, ki_new, cache_idx_t)


def _topk_mask_kernel(k_sel, sc_ref, mask_ref):
    n_tiles, bsz, _ = sc_ref.shape
    width = n_tiles * LANES
    ncol = n_tiles + 0 * pl.program_id(0)
    shape = (bsz, LANES)
    kf = jnp.full((bsz, 1), float(k_sel), F32)
    thr, n_gt, n_ge = _kth_largest_rows(sc_ref, 0, ncol, kf)
    cut = lax.cond(jnp.max(n_ge - kf) > 0.0,
                   lambda: jnp.where(n_ge > kf, _tie_cutoff_rows(sc_ref, 0, ncol, thr, kf - n_gt), width),
                   lambda: jnp.full((bsz, 1), width, jnp.int32))
    tb, cb = jnp.broadcast_to(thr, shape), jnp.broadcast_to(cut, shape)
    lane = lax.broadcasted_iota(jnp.int32, shape, 1)

    def mark(c, carry):
        v = sc_ref[c]
        sel = jnp.logical_or(v > tb, jnp.logical_and(v == tb, lane + c * LANES <= cb))
        mask_ref[c] = jnp.where(sel, 1.0, 0.0)
        return carry
    lax.fori_loop(0, ncol, mark, 0)


def _topk_mask(scores_t, k_sel):
    assert (scores_t.shape[0] * LANES) % DSA_TK == 0
    whole = pl.BlockSpec(scores_t.shape, lambda i: (0, 0, 0))
    return pl.pallas_call(
        functools.partial(_topk_mask_kernel, k_sel), grid=(1,),
        in_specs=[whole], out_specs=whole,
        out_shape=jax.ShapeDtypeStruct(scores_t.shape, F32),
        compiler_params=_cparams(("arbitrary",)), name="dsa_sample_topk",
    )(scores_t)


SAMPLE_PAGES_PER_STEP = 16


def _dsa_sample_attn_kernel(pt_ref, qt_ref, knt_ref, vnt_ref, mask_ref, ck_ref, cv_ref, o_ref,
                            kbuf, vbuf, sem_ref, m_ref, l_ref, acc_ref):
    b, c = pl.program_id(0), pl.program_id(1)
    nb, nc = pl.num_programs(0), pl.num_programs(1)
    pps = SAMPLE_PAGES_PER_STEP
    n_pages = pt_ref.shape[1]
    step = b * nc + c

    def issue(s, slot):
        bb, cc = s // nc, s % nc
        for i in range(pps):
            page = pt_ref[bb, cc * pps + i]
            pltpu.make_async_copy(ck_ref.at[0, page], kbuf.at[slot, i], sem_ref.at[0, slot]).start()
            pltpu.make_async_copy(cv_ref.at[0, page], vbuf.at[slot, i], sem_ref.at[1, slot]).start()

    @pl.when(step == 0)
    def _():
        issue(0, 0)

    @pl.when(step + 1 < nb * nc)
    def _():
        issue(step + 1, (step + 1) % 2)

    slot = step % 2
    pltpu.make_async_copy(kbuf.at[slot], kbuf.at[slot], sem_ref.at[0, slot]).wait()
    pltpu.make_async_copy(vbuf.at[slot], vbuf.at[slot], sem_ref.at[1, slot]).wait()

    @pl.when(c == 0)
    def _():
        m_ref[...] = jnp.full(m_ref.shape, NEG_BIG, F32)
        l_ref[...] = jnp.zeros(l_ref.shape, F32)
        acc_ref[...] = jnp.zeros(acc_ref.shape, F32)

    keep = mask_ref[0, pl.ds(pl.multiple_of(c * pps, pps), pps), :] > 0.0
    qt = qt_ref[0].astype(F32)
    heads = range(A_HEADS)
    qcols = [jnp.broadcast_to(qt[:, h:h + 1], (HEAD_DIM, PAGE_SIZE)) for h in heads]
    lg = [jnp.concatenate([jnp.sum(kbuf[slot, i, h] * qcols[h], axis=0, keepdims=True) for i in range(pps)], axis=0)
          for h in heads]
    lg = [jnp.where(keep, x, NEG_BIG) for x in lg]
    m_old = [m_ref[h] for h in heads]
    m_new = [jnp.maximum(m_old[h], jnp.max(lg[h], axis=(0, 1), keepdims=True)) for h in heads]
    alpha = [jnp.exp(m_old[h] - m_new[h]) for h in heads]
    p = [jnp.where(keep, jnp.exp(lg[h] - m_new[h][0:1, :]), 0.0) for h in heads]
    for h in heads:
        l_ref[h] = alpha[h] * l_ref[h] + jnp.sum(p[h], axis=(0, 1), keepdims=True)
        m_ref[h] = m_new[h]
    for h in heads:
        pv = vbuf[slot, 0, h] * p[h][0:1, :]
        for i in range(1, pps):
            pv = pv + vbuf[slot, i, h] * p[h][i:i + 1, :]
        acc_ref[h] = alpha[h][0:1, :] * acc_ref[h] + pv

    @pl.when(c == nc - 1)
    def _():
        lane = lax.broadcasted_iota(jnp.int32, (HEAD_DIM, LANES), 1)
        keep_new = mask_ref[0, n_pages:n_pages + 1, 0:1] > 0.0
        out = jnp.zeros((HEAD_DIM, LANES), F32)
        for h in range(A_HEADS):
            lg_new = jnp.sum(knt_ref[0][:, h:h + 1] * qt[:, h:h + 1], axis=0, keepdims=True)
            lg_new = jnp.where(keep_new, lg_new, NEG_BIG)
            m_old = m_ref[h][0:1, 0:1]
            m_new = jnp.maximum(m_old, lg_new)
            alpha = jnp.exp(m_old - m_new)
            p_new = jnp.where(keep_new, jnp.exp(lg_new - m_new), 0.0)
            l_fin = alpha * l_ref[h][0:1, 0:1] + p_new
            col = alpha * jnp.sum(acc_ref[h], axis=-1, keepdims=True) + p_new * vnt_ref[0][:, h:h + 1]
            out = jnp.where(lane == h, col / l_fin, out)
        o_ref[0] = out


def _dsa_sample_attn(page_table, q_t, k_new_t, v_new_t, mask3, cache_k_t, cache_v_t):
    bsz, n_pages = page_table.shape
    pps = SAMPLE_PAGES_PER_STEP
    assert n_pages % pps == 0
    col = lambda: pl.BlockSpec((1, HEAD_DIM, A_HEADS), lambda b, c, pt: (b, 0, 0))
    page_buf = pltpu.VMEM((2, pps, A_HEADS, HEAD_DIM, PAGE_SIZE), F32)
    return pl.pallas_call(
        _dsa_sample_attn_kernel,
        grid_spec=pltpu.PrefetchScalarGridSpec(
            num_scalar_prefetch=1, grid=(bsz, n_pages // pps),
            in_specs=[col(), col(), col(), pl.BlockSpec((1,) + mask3.shape[1:], lambda b, c, pt: (b, 0, 0)),
                      pl.BlockSpec(memory_space=pl.ANY), pl.BlockSpec(memory_space=pl.ANY)],
            out_specs=pl.BlockSpec((1, HEAD_DIM, LANES), lambda b, c, pt: (b, 0, 0)),
            scratch_shapes=[page_buf, page_buf, pltpu.SemaphoreType.DMA((2, 2)),
                            pltpu.VMEM((A_HEADS, SUBLANES, LANES), F32), pltpu.VMEM((A_HEADS, SUBLANES, LANES), F32),
                            pltpu.VMEM((A_HEADS, HEAD_DIM, LANES), F32)]),
        out_shape=jax.ShapeDtypeStruct((bsz, HEAD_DIM, LANES), F32),
        compiler_params=_cparams(("arbitrary", "arbitrary")), name="dsa_sample_attention",
    )(page_table, q_t, k_new_t, v_new_t, mask3, cache_k_t, cache_v_t)


def _head_ones():
    h = np.arange(B_WIDTH) // B_HEAD_DIM
    return jnp.asarray((h[:, None] == h[None, :]).astype(np.float32))


def _rwkv_pre_kernel(sh_ref, w0_ref, wup_ref, a0_ref, aup_ref, gup_ref, kk_ref_, ka_ref, g1_ref,
                     ld_ref, kk_ref, kb_ref, kp_ref, g_ref):
    k = sh_ref[:, B_WIDTH:2 * B_WIDTH]
    o = 3 * B_WIDTH
    xw = sh_ref[:, o:o + DECAY_LORA]
    xa = sh_ref[:, o + DECAY_LORA:o + DECAY_LORA + AAA_LORA]
    xg = sh_ref[:, o + DECAY_LORA + AAA_LORA:SHIFT_W]
    w = w0_ref[...] + _dotx(jnp.tanh(xw), wup_ref[...])
    logw = -jax.nn.softplus(-w) - 0.5
    ld_ref[...] = -jnp.exp(logw)
    a = jax.nn.sigmoid(a0_ref[...] + _dotx(xa, aup_ref[...]))
    g_ref[...] = _dotx(jax.nn.sigmoid(xg), gup_ref[...])
    kk = k * kk_ref_[...]
    kk = kk / jnp.maximum(jnp.sqrt(_dotx(kk * kk, g1_ref[...])), 1e-12)
    kk_ref[...] = kk
    kb_ref[...] = kk * a
    kp_ref[...] = k * (1.0 + (a - 1.0) * ka_ref[...])


def _rwkv_pre(shifted, w0, w_lora_up, a0, a_lora_up, g_lora_up, k_k, k_a, tm):
    t = shifted.shape[0]
    row = lambda w: pl.BlockSpec((tm, w), lambda i: (i, 0))
    const = lambda a: pl.BlockSpec(a.shape, lambda i: (0, 0))
    g1 = _head_ones()
    args = (shifted, w0, w_lora_up, a0, a_lora_up, g_lora_up, k_k, k_a, g1)
    out = jax.ShapeDtypeStruct((t, B_WIDTH), F32)
    return pl.pallas_call(
        _rwkv_pre_kernel, grid=(t // tm,),
        in_specs=[row(SHIFT_W)] + [const(a) for a in args[1:]],
        out_specs=(row(B_WIDTH),) * 5, out_shape=(out,) * 5,
        compiler_params=_cparams(("arbitrary",)), name="rwkv_prep",
    )(*args)


def _rwkv_seq_kernel(r_ref, v_ref, ld_ref, kk_ref, kb_ref, kp_ref, y_ref, so_ref, st_ref):
    c = CHUNK

    @pl.when(pl.program_id(0) == 0)
    def _():
        st_ref[...] = jnp.zeros(st_ref.shape, F32)

    ri = lax.broadcasted_iota(jnp.int32, (c, c), 0)
    ci = lax.broadcasted_iota(jnp.int32, (c, c), 1)
    incl, strict, eye = ri >= ci, ri > ci, ri == ci
    ld = ld_ref[...]
    tri = jnp.where(incl, 1.0, 0.0).astype(BF16)
    ld_hi = ld.astype(BF16)
    rest = ld - ld_hi.astype(F32)
    ld_mid = rest.astype(BF16)
    ld_lo = (rest - ld_mid.astype(F32)).astype(BF16)
    cum = _dot(tri, ld_hi) + (_dot(tri, ld_mid) + _dot(tri, ld_lo))
    cum_c = cum[c - 1:c, :]
    e_neg, e_rem = jnp.exp(-cum), jnp.exp(cum_c - cum)
    kk, kb, kp = kk_ref[...], kb_ref[...], kp_ref[...]
    a_all = -kk * jnp.exp(cum - ld)
    r_all = r_ref[...] * jnp.exp(cum)
    bt_all, kt_all = kb * e_neg, kp * e_neg
    bh_all, kh_all = kb * e_rem, kp * e_rem
    pc = jnp.exp(cum_c)
    v_all = v_ref[...]
    n = B_HEAD_DIM
    heads = range(B_HEADS)
    hsl = [slice(h * n, (h + 1) * n) for h in heads]
    p = [_dot3_t(jnp.concatenate([a_all[:, s], r_all[:, s]], 0), jnp.concatenate([bt_all[:, s], kt_all[:, s]], 0))
         for s in hsl]
    l_ak = [jnp.where(strict, q[:c, c:], 0.0) for q in p]
    m_rb = [jnp.where(incl, q[c:, :c], 0.0) for q in p]
    m_rk = [jnp.where(incl, q[c:, c:], 0.0) for q in p]
    lp = [jnp.where(strict, q[:c, :c], 0.0) for q in p]
    x = [jnp.concatenate([a_all[:, s], _dot3(l_ak[h], v_all[:, s])], 1) for h, s in enumerate(hsl)]
    n_dbl = int(np.log2(c))
    for it in range(n_dbl):
        x = [x[h] + _dot3(lp[h], x[h]) for h in heads]
        if it + 1 < n_dbl:
            lp = [_dot3(lp[h], lp[h]) for h in heads]
    bm = [_dot3(jnp.concatenate([bh_all[:, s].T, m_rb[h]], 0), x[h]) for h, s in enumerate(hsl)]
    kv = [_dot3(jnp.concatenate([kh_all[:, s].T, m_rk[h]], 0), v_all[:, s]) for h, s in enumerate(hsl)]
    for h, s in enumerate(hsl):
        m_c = jnp.where(eye, jnp.broadcast_to(pc[:, s], (n, n)), 0.0) + bm[h][:n, :n]
        g_c = r_all[:, s] + bm[h][n:, :n]
        gs = _dot3(jnp.concatenate([g_c, m_c], 0), st_ref[h])
        y_ref[:, s] = gs[:c, :] + (bm[h][n:, n:] + kv[h][n:, :])
        st_ref[h] = gs[c:, :] + (bm[h][:n, n:] + kv[h][:n, :])
    so_ref[...] = st_ref[...]


def _rwkv_seq(shifted, ld, kk, kb, kp):
    t = shifted.shape[0]
    assert t % CHUNK == 0
    col = lambda j: pl.BlockSpec((CHUNK, B_WIDTH), lambda i: (i, j))
    state = jax.ShapeDtypeStruct((B_HEADS, B_HEAD_DIM, B_HEAD_DIM), F32)
    return pl.pallas_call(
        _rwkv_seq_kernel, grid=(t // CHUNK,),
        in_specs=[col(0), col(2), col(0), col(0), col(0), col(0)],
        out_specs=(col(0), pl.BlockSpec(state.shape, lambda i: (0, 0, 0))),
        out_shape=(jax.ShapeDtypeStruct((t, B_WIDTH), F32), state),
        scratch_shapes=[pltpu.VMEM(state.shape, F32)],
        compiler_params=_cparams(("arbitrary",)), name="rwkv_chunked_scan",
    )(shifted, shifted, ld, kk, kb, kp)


RWKV_STEP_ROWS = 8


def _rwkv_step_kernel(s_ref, r_ref, ld_ref, kk_ref, kb_ref, kp_ref, v_ref, so_ref, y_ref):
    def one(b, carry):
        row = pl.ds(b, 1)
        r, w, kk, kb, kp = r_ref[row, :], jnp.exp(ld_ref[row, :]), kk_ref[row, :], kb_ref[row, :], kp_ref[row, :]
        for h in range(B_HEADS):
            hs = slice(h * B_HEAD_DIM, (h + 1) * B_HEAD_DIM)
            s = s_ref[b, h]
            sa = -jnp.sum(s * kk[:, hs], axis=-1, keepdims=True)
            s2 = s * w[:, hs] + sa * kb[:, hs] + v_ref[b, h] * kp[:, hs]
            so_ref[b, h] = s2
            y_ref[b, h] = jnp.sum(s2 * r[:, hs], axis=-1, keepdims=True)
        return carry
    lax.fori_loop(0, s_ref.shape[0], one, 0)


def _rwkv_step(state, shifted, ld, kk, kb, kp, v_col):
    bsz = state.shape[0]
    nb = RWKV_STEP_ROWS
    assert bsz % nb == 0
    row = lambda: pl.BlockSpec((nb, B_WIDTH), lambda i: (i, 0))
    st = pl.BlockSpec((nb, B_HEADS, B_HEAD_DIM, B_HEAD_DIM), lambda i: (i, 0, 0, 0))
    colv = pl.BlockSpec((nb, B_HEADS, B_HEAD_DIM, 1), lambda i: (i, 0, 0, 0))
    return pl.pallas_call(
        _rwkv_step_kernel, grid=(bsz // nb,),
        in_specs=[st, row(), row(), row(), row(), row(), colv],
        out_specs=(st, colv),
        out_shape=(jax.ShapeDtypeStruct(state.shape, F32),
                   jax.ShapeDtypeStruct((bsz, B_HEADS, B_HEAD_DIM, 1), F32)),
        compiler_params=_cparams(("arbitrary",)), name="rwkv_single_step",
    )(state, shifted, ld, kk, kb, kp, v_col)


def _rwkv_post_kernel(y_ref, r_ref, v_ref, kp_ref, g_ref, rk_ref, gng_ref, gnb_ref, g1_ref, o_ref):
    g1 = g1_ref[...]
    inv_n = 1.0 / B_HEAD_DIM
    y = y_ref[...]
    d = y - _dotx(y, g1) * inv_n
    var = _dotx(d * d, g1) * inv_n
    yn = d * lax.rsqrt(var + GN_EPS) * gng_ref[...] + gnb_ref[...]
    bonus = _dotx(r_ref[...] * kp_ref[...] * rk_ref[...], g1) * v_ref[...]
    o_ref[...] = (yn + bonus) * g_ref[...]


def _rwkv_post(y, shifted, kp, g, r_k, gn_g, gn_b, tm):
    t = y.shape[0]
    col = lambda j: pl.BlockSpec((tm, B_WIDTH), lambda i: (i, j))
    const = lambda a: pl.BlockSpec(a.shape, lambda i: (0, 0))
    g1 = _head_ones()
    return pl.pallas_call(
        _rwkv_post_kernel, grid=(t // tm,),
        in_specs=[col(0), col(0), col(2), col(0), col(0), const(r_k), const(gn_g), const(gn_b), const(g1)],
        out_specs=col(0), out_shape=jax.ShapeDtypeStruct((t, B_WIDTH), F32),
        compiler_params=_cparams(("arbitrary",)), name="rwkv_output_norm",
    )(y, shifted, shifted, kp, g, r_k, gn_g, gn_b, g1)


def _outproj_kernel(x_ref, attn_ref, rw_ref, gate_ref, wa_ref, wb_ref, wo_ref, nf_ref, wr_ref, br_ref,
                    h_ref, hn_ref, comb_ref):
    ga, gb = gate_ref[:, :D_MODEL], gate_ref[:, D_MODEL:]
    merged = ga * _dot(attn_ref[...].astype(BF16), wa_ref[...]) + gb * _dot(rw_ref[...].astype(BF16), wb_ref[...])
    h = x_ref[...] + _dot(merged.astype(BF16), wo_ref[...])
    h_ref[...] = h
    hn = _rms(h, nf_ref[...])
    hn_ref[...] = hn.astype(BF16)
    logits = _dotx(hn, wr_ref[...]) + br_ref[...]
    lane = lax.broadcasted_iota(jnp.int32, logits.shape, 1)
    big = jnp.int32(LANES)
    first = lambda hit: jnp.min(jnp.where(hit, lane, big), axis=-1, keepdims=True)
    is_g = lane < N_GROUPS
    gl = jnp.where(is_g, logits, -jnp.inf)
    gmax = jnp.max(gl, axis=-1, keepdims=True)
    p_group = 1.0 / jnp.sum(jnp.exp(gl - gmax), axis=-1, keepdims=True)
    g_sel = first(gl == gmax)
    e0 = N_GROUPS + g_sel * EXPERTS_PER_GROUP
    el = jnp.where(jnp.logical_and(lane >= e0, lane < e0 + EXPERTS_PER_GROUP), logits, -jnp.inf)
    v1 = jnp.max(el, axis=-1, keepdims=True)
    i1 = first(el == v1)
    el2 = jnp.where(lane == i1, -jnp.inf, el)
    v2 = jnp.max(el2, axis=-1, keepdims=True)
    i2 = first(el2 == v2)
    e21 = jnp.exp(v2 - v1)
    w1 = p_group / (1.0 + e21)
    w2 = p_group * e21 / (1.0 + e21)
    comb_ref[...] = jnp.where(lane == i1 - N_GROUPS, w1, 0.0) + jnp.where(lane == i2 - N_GROUPS, w2, 0.0)


def _outproj(x, attn, rw, gates, wa, wb, wo, norm_ffn, w_router, b_router, tm):
    t = x.shape[0]
    row = lambda w: pl.BlockSpec((tm, w), lambda i: (i, 0))
    const = lambda a: pl.BlockSpec(a.shape, lambda i: (0, 0))
    return pl.pallas_call(
        _outproj_kernel, grid=(t // tm,),
        in_specs=[row(D_MODEL), row(A_WIDTH), row(B_WIDTH), row(2 * D_MODEL), const(wa), const(wb), const(wo),
                  const(norm_ffn), const(w_router), const(b_router)],
        out_specs=(row(D_MODEL), row(D_MODEL), row(LANES)),
        out_shape=(jax.ShapeDtypeStruct((t, D_MODEL), F32), jax.ShapeDtypeStruct((t, D_MODEL), BF16),
                   jax.ShapeDtypeStruct((t, LANES), F32)),
        compiler_params=_cparams(("arbitrary",)), name="outproj_router",
    )(x, attn, rw, gates, wa, wb, wo, norm_ffn, w_router, b_router)


def _moe_kernel(h_ref, hn_ref, comb_ref, wg_ref, wu_ref, wd_ref, nfin_ref, o_ref, acc_ref):
    e = pl.program_id(1)
    hn = hn_ref[...]
    act = jax.nn.silu(_dot(hn, wg_ref[0])) * _dot(hn, wu_ref[0])
    y = _dot(act.astype(BF16), wd_ref[0])
    lane = lax.broadcasted_iota(jnp.int32, comb_ref.shape, 1)
    c_e = jnp.sum(jnp.where(lane == e, comb_ref[...], 0.0), axis=-1, keepdims=True)

    @pl.when(e == 0)
    def _():
        acc_ref[...] = c_e * y

    @pl.when(e > 0)
    def _():
        acc_ref[...] += c_e * y

    @pl.when(e == pl.num_programs(1) - 1)
    def _():
        o_ref[...] = _rms(h_ref[...] + acc_ref[...], nfin_ref[...])


def _moe(h, hn, comb, wg, wu, wd, norm_final, tm):
    t = h.shape[0]
    row = lambda w: pl.BlockSpec((tm, w), lambda i, e: (i, 0))
    return pl.pallas_call(
        _moe_kernel, grid=(t // tm, N_EXPERTS),
        in_specs=[row(D_MODEL), row(D_MODEL), row(LANES),
                  pl.BlockSpec((1, D_MODEL, D_EXPERT), lambda i, e: (e, 0, 0)),
                  pl.BlockSpec((1, D_MODEL, D_EXPERT), lambda i, e: (e, 0, 0)),
                  pl.BlockSpec((1, D_EXPERT, D_MODEL), lambda i, e: (e, 0, 0)),
                  pl.BlockSpec(norm_final.shape, lambda i, e: (0, 0))],
        out_specs=row(D_MODEL), out_shape=jax.ShapeDtypeStruct((t, D_MODEL), F32),
        scratch_shapes=[pltpu.VMEM((tm, D_MODEL), F32)],
        compiler_params=_cparams(("arbitrary", "arbitrary")), name="moe_experts",
    )(h, hn, comb, wg, wu, wd, norm_final)


def _tile(t, pref):
    return min(pref, t)


def _mixer_tail(x, attn, shifted, gates, state_step, weights, ffn):
    (w0, w_lora_up, a0, a_lora_up, g_lora_up, k_k, k_a, r_k, gn_g, gn_b, wa, wb, wo) = weights
    (norm_ffn, w_router, b_router, wg, wu, wd, norm_final) = ffn
    t = x.shape[0]
    tm = _tile(t, 256)
    ld, kk, kb, kp, g = _rwkv_pre(shifted, w0, w_lora_up, a0, a_lora_up, g_lora_up, k_k, k_a, tm)
    if state_step is None:
        y, st = _rwkv_seq(shifted, ld, kk, kb, kp)
        wkv = jnp.swapaxes(st, 1, 2)[None]
    else:
        v_col = shifted[:, 2 * B_WIDTH:3 * B_WIDTH].reshape(t, B_HEADS, B_HEAD_DIM, 1)
        wkv, y_col = _rwkv_step(state_step, shifted, ld, kk, kb, kp, v_col)
        y = y_col.reshape(t, B_WIDTH)
    rw = _rwkv_post(y, shifted, kp, g, r_k, gn_g, gn_b, tm)
    h, hn, comb = _outproj(x, attn, rw, gates, wa, wb, wo, norm_ffn, w_router, b_router, tm)
    out = _moe(h, hn, comb, wg, wu, wd, norm_final, _tile(t, 1024))
    return out, wkv


def kernel(x_prompt, x_sample, cache_k, cache_v, cache_idx_k, state_wkv, state_shift, page_table, norm_mix, w_in, mu_shift, w0, w_lora_up, a0, a_lora_up, g_lora_up, k_k, k_a, r_k, gn_g, gn_b, w_branch_a, w_branch_b, w_out, norm_ffn, w_router_group, b_router_group, w_router_expert, b_router_expert, w_gate, w_up, w_down, norm_final):
    assert w_in.shape[0] == 1, "single-layer kernel"
    bp, tp, _ = x_prompt.shape
    bs, ts, _ = x_sample.shape
    assert bp == 1 and ts == 1
    n_pages = page_table.shape[1]
    past = n_pages * PAGE_SIZE
    row2 = lambda a: a.reshape(1, -1)

    w_packed = _pack_w_in(w_in[0])
    mix_w = (row2(w0[0]), w_lora_up[0], row2(a0[0]), a_lora_up[0], g_lora_up[0], row2(k_k[0]), row2(k_a[0]),
             row2(r_k[0]), row2(gn_g[0]), row2(gn_b[0]),
             w_branch_a[0].astype(BF16), w_branch_b[0].astype(BF16), w_out[0].astype(BF16))
    pad_r = jnp.zeros((D_MODEL, LANES - N_GROUPS - N_EXPERTS), F32)
    w_router = jnp.concatenate([w_router_group[0], w_router_expert[0], pad_r], 1)
    b_router = jnp.concatenate([b_router_group[0], b_router_expert[0], pad_r[0]])[None]
    ffn_w = (row2(norm_ffn[0]), w_router, b_router, w_gate[0].astype(BF16), w_up[0].astype(BF16),
             w_down[0].astype(BF16), row2(norm_final))
    g_mix, mu = row2(norm_mix[0]), row2(mu_shift[0])

    xp = x_prompt[0]
    tabs = _rope_tables(jnp.arange(tp))
    (q, kf, kb16, vf, vb16, qi, kiwi, ki16, shifted, shift_last, gates) = _inproj(
        xp, g_mix, w_packed, *tabs, mu, jnp.zeros((1, SHIFT_W), F32), True, _tile(tp, 256))
    attn = _dsa_prompt(q, kb16, vb16, qi, kiwi, ki16)
    y_p, wkv_p = _mixer_tail(xp, attn, shifted, gates, None, mix_w, ffn_w)

    xs = x_sample[:, 0]
    tabs_s = _rope_tables(jnp.full((bs,), past, jnp.int32))
    (q_s, kf_s, _, vf_s, _, qi_s, kiwi_s, ki16_s, shifted_s, shift_raw_s, gates_s) = _inproj(
        xs, g_mix, w_packed, *tabs_s, mu, state_shift[0], False, _tile(bs, 256))
    pad_h = IDX_ROWS - IDX_HEADS
    qi3 = jnp.pad(qi_s.reshape(bs, IDX_HEADS, IDX_DIM), ((0, 0), (0, pad_h), (0, 0)))
    wi3 = jnp.pad(kiwi_s[:, IDX_DIM:IDX_DIM + IDX_HEADS], ((0, 0), (0, pad_h)))[:, :, None]
    scores = _dsa_sample_scores(page_table, qi3, wi3, ki16_s[:, None, :], jnp.swapaxes(cache_idx_k, 2, 3))
    k_sel = min(TOPK_MAX, (past + ts) // 4)
    mask = jnp.swapaxes(_topk_mask(jnp.swapaxes(scores, 0, 1), k_sel), 0, 1)
    cols = lambda a: jnp.swapaxes(a.reshape(bs, A_HEADS, HEAD_DIM), 1, 2)
    page_t = lambda cache: jnp.transpose(cache, (0, 1, 3, 4, 2))
    attn_t = _dsa_sample_attn(page_table, cols(q_s), cols(kf_s), cols(vf_s), mask, page_t(cache_k), page_t(cache_v))
    attn_s = jnp.swapaxes(attn_t[:, :, :A_HEADS], 1, 2).reshape(bs, A_WIDTH)
    y_s, wkv_s = _mixer_tail(xs, attn_s, shifted_s, gates_s, state_wkv[0], mix_w, ffn_w)

    kv5 = lambda a, b, t: a.reshape(1, b, t, A_HEADS, HEAD_DIM)
    return (y_p[None], y_s[:, None],
            kv5(kf, 1, tp), kv5(vf, 1, tp), kiwi[:, :IDX_DIM].reshape(1, 1, tp, IDX_DIM),
            wkv_p[None], shift_last.reshape(1, 1, SHIFT_W),
            kv5(kf_s, bs, 1), kv5(vf_s, bs, 1), kiwi_s[:, :IDX_DIM].reshape(1, bs, 1, IDX_DIM),
            wkv_s[None], shift_raw_s[None])
```

```python
import functools

import jax
import jax.numpy as jnp
import numpy as np
from jax import lax
from jax.experimental import pallas as pl
from jax.experimental.pallas import tpu as pltpu

F32 = jnp.float32
BF16 = jnp.bfloat16

D_MODEL = 1024
PAGE_SIZE = 128
A_HEADS = 8
HEAD_DIM = 64
A_WIDTH = A_HEADS * HEAD_DIM
IDX_HEADS = 4
IDX_DIM = 64
TOPK_MAX = 256
ROT_DIM = HEAD_DIM // 4
ROPE_THETA = 500000.0
B_HEADS = 8
B_HEAD_DIM = 64
B_WIDTH = B_HEADS * B_HEAD_DIM
DECAY_LORA = 64
AAA_LORA = 64
GATE_LORA = 128
GN_EPS = B_HEAD_DIM * 1e-5
SHIFT_W = 3 * B_WIDTH + DECAY_LORA + AAA_LORA + GATE_LORA
N_GROUPS = 4
EXPERTS_PER_GROUP = 8
N_EXPERTS = N_GROUPS * EXPERTS_PER_GROUP
D_EXPERT = 512
RMS_EPS = 1e-6

LANES = 128
SUBLANES = 8
VMEM_LIMIT_BYTES = 56 * 1024 * 1024

QKV_END = 3 * A_WIDTH
QI_END = QKV_END + IDX_HEADS * IDX_DIM
KIWI_END = QI_END + LANES
SHIFT_END = KIWI_END + SHIFT_W
D_IN_PACKED = SHIFT_END + 2 * D_MODEL
IDX_W_SCALE = IDX_HEADS ** -0.5 * IDX_DIM ** -0.5
NEG_BIG = -1e30
CHUNK = 64


def _cparams(sem):
    return pltpu.CompilerParams(dimension_semantics=sem, vmem_limit_bytes=VMEM_LIMIT_BYTES)


def _dot(a, b):
    return jnp.dot(a, b, preferred_element_type=F32)


def _dot_t(a, b):
    return lax.dot_general(a, b, (((1,), (1,)), ((), ())), preferred_element_type=F32)


def _dotx(a, b):
    return jnp.dot(a, b, preferred_element_type=F32, precision=lax.Precision.HIGHEST)


def _dotx_t(a, b):
    return lax.dot_general(a, b, (((1,), (1,)), ((), ())), preferred_element_type=F32,
                           precision=lax.Precision.HIGHEST)


def _split2(a):
    hi = a.astype(BF16)
    return hi, (a - hi.astype(F32)).astype(BF16)


def _dot3(a, b):
    ah, al = _split2(a)
    bh, bl = _split2(b)
    return _dot(ah, bh) + (_dot(ah, bl) + _dot(al, bh))


def _dot3_t(a, b):
    ah, al = _split2(a)
    bh, bl = _split2(b)
    return _dot_t(ah, bh) + (_dot_t(ah, bl) + _dot_t(al, bh))


def _rms(x, g):
    return x * lax.rsqrt(jnp.mean(x * x, axis=-1, keepdims=True) + RMS_EPS) * g


def _rope_slab(x, c, sa, sb):
    return x * c + pltpu.roll(x, ROT_DIM // 2, 1) * sa + pltpu.roll(x, LANES - ROT_DIM // 2, 1) * sb


def _inproj_kernel(seq_mode, x_ref, g_ref, w_ref, c_ref, sa_ref, sb_ref, mu_ref, prev_ref,
                   q_ref, kf_ref, kb_ref, vf_ref, vb_ref, qi_ref, kiwi_ref, kib_ref, wrep_ref, sh_ref, last_ref,
                   gate_ref, carry_ref):
    tm = x_ref.shape[0]
    xn = _rms(x_ref[...], g_ref[...]).astype(BF16)
    c, sa, sb = c_ref[...], sa_ref[...], sb_ref[...]

    def proj(c0, c1):
        return _dot(xn, w_ref[:, c0:c1])

    def rope(z):
        return jnp.concatenate(
            [_rope_slab(z[:, s:s + LANES], c, sa, sb) for s in range(0, z.shape[1], LANES)], axis=1)

    q = rope(proj(0, A_WIDTH)) * (HEAD_DIM ** -0.5)
    k = rope(proj(A_WIDTH, 2 * A_WIDTH))
    kf_ref[...] = k
    v = proj(2 * A_WIDTH, QKV_END)
    vf_ref[...] = v
    lane_h = lax.broadcasted_iota(jnp.int32, (tm, LANES - HEAD_DIM), 1)
    ones_pad = jnp.where(lane_h == 0, 1.0, 0.0)
    for h in range(A_HEADS):
        hs = slice(h * HEAD_DIM, (h + 1) * HEAD_DIM)
        q_ref[h] = q[:, hs].astype(BF16)
        kb_ref[h] = k[:, hs].astype(BF16)
        vb_ref[h] = jnp.concatenate([v[:, hs], ones_pad], axis=1).astype(BF16)
    qi_ref[...] = rope(proj(QKV_END, QI_END)).astype(BF16)
    kiwi = proj(QI_END, KIWI_END)
    lane = lax.broadcasted_iota(jnp.int32, kiwi.shape, 1)
    kiwi = jnp.where(lane < IDX_DIM, _rope_slab(kiwi, c, sa, sb), kiwi * IDX_W_SCALE)
    kiwi_ref[...] = kiwi
    kib_ref[...] = kiwi[:, :IDX_DIM].astype(BF16)
    wrep_ref[...] = jnp.concatenate(
        [jnp.broadcast_to(kiwi[:, IDX_DIM + h:IDX_DIM + h + 1], (tm, LANES)) for h in range(IDX_HEADS)], axis=1)

    u = proj(KIWI_END, SHIFT_END)
    if seq_mode:
        @pl.when(pl.program_id(0) == 0)
        def _():
            carry_ref[...] = prev_ref[...]
        row = lax.broadcasted_iota(jnp.int32, u.shape, 0)
        u_prev = jnp.where(row == 0, carry_ref[...], pltpu.roll(u, 1, 0))
        carry_ref[...] = u[tm - 1:tm, :]
    else:
        u_prev = prev_ref[...]
    sh_ref[...] = u + (u_prev - u) * mu_ref[...]
    last_ref[...] = u[tm - 1:tm, :] if seq_mode else u
    gate_ref[...] = jax.nn.sigmoid(proj(SHIFT_END, D_IN_PACKED))


def _inproj(x, norm_g, w_packed, rope_c, rope_sa, rope_sb, mu, prev, seq_mode, tm):
    t = x.shape[0]
    assert t % tm == 0
    row = lambda w: pl.BlockSpec((tm, w), lambda i: (i, 0))
    const = lambda a: pl.BlockSpec(a.shape, lambda i: (0, 0))
    prev_spec = const(prev) if seq_mode else row(SHIFT_W)
    head = lambda w: pl.BlockSpec((A_HEADS, tm, w), lambda i: (0, i, 0))
    out_shape = (
        jax.ShapeDtypeStruct((A_HEADS, t, HEAD_DIM), BF16),
        jax.ShapeDtypeStruct((t, A_WIDTH), F32), jax.ShapeDtypeStruct((A_HEADS, t, HEAD_DIM), BF16),
        jax.ShapeDtypeStruct((t, A_WIDTH), F32), jax.ShapeDtypeStruct((A_HEADS, t, LANES), BF16),
        jax.ShapeDtypeStruct((t, IDX_HEADS * IDX_DIM), BF16),
        jax.ShapeDtypeStruct((t, LANES), F32),
        jax.ShapeDtypeStruct((t, IDX_DIM), BF16),
        jax.ShapeDtypeStruct((t, IDX_HEADS * LANES), F32),
        jax.ShapeDtypeStruct((t, SHIFT_W), F32),
        jax.ShapeDtypeStruct((1 if seq_mode else t, SHIFT_W), F32),
        jax.ShapeDtypeStruct((t, 2 * D_MODEL), F32),
    )
    out_specs = (head(HEAD_DIM), row(A_WIDTH), head(HEAD_DIM), row(A_WIDTH), head(LANES),
                 row(IDX_HEADS * IDX_DIM), row(LANES), row(IDX_DIM), row(IDX_HEADS * LANES), row(SHIFT_W),
                 pl.BlockSpec((1, SHIFT_W), lambda i: (0, 0)) if seq_mode else row(SHIFT_W), row(2 * D_MODEL))
    return pl.pallas_call(
        functools.partial(_inproj_kernel, seq_mode),
        grid=(t // tm,),
        in_specs=[row(D_MODEL), const(norm_g), const(w_packed), row(LANES), row(LANES), row(LANES),
                  const(mu), prev_spec],
        out_specs=out_specs, out_shape=out_shape,
        scratch_shapes=[pltpu.VMEM((1, SHIFT_W), F32)],
        compiler_params=_cparams(("arbitrary",)),
        name="inproj_seq" if seq_mode else "inproj_batch",
    )(x, norm_g, w_packed, rope_c, rope_sa, rope_sb, mu, prev)


def _rope_tables(pos):
    half = ROT_DIM // 2
    inv = ROPE_THETA ** (-jnp.arange(half, dtype=F32) / half)
    ang = pos.astype(F32)[:, None] * inv[None, :]
    cos, sin = jnp.cos(ang), jnp.sin(ang)
    t = pos.shape[0]
    pad = jnp.zeros((t, HEAD_DIM - ROT_DIM), F32)
    zero = jnp.zeros((t, half), F32)
    c = jnp.concatenate([cos, cos, pad + 1.0], 1)
    sa = jnp.concatenate([zero, sin, pad], 1)
    sb = jnp.concatenate([-sin, zero, pad], 1)
    rep = lambda a: jnp.concatenate([a] * (LANES // HEAD_DIM), 1)
    return rep(c), rep(sa), rep(sb)


def _pack_w_in(w_in):
    lead = QI_END + IDX_DIM + IDX_HEADS
    pad = jnp.zeros((D_MODEL, KIWI_END - lead), w_in.dtype)
    return jnp.concatenate([w_in[:, :lead], pad, w_in[:, lead:]], 1).astype(BF16)


DSA_TQ = 256
DSA_TK = 512
THR_ROWS = 128
THR_WARM_STEPS = 22
THR_SLOT_TARGET = 12
THR_CHEAP_STEPS = 1
THR_UNROLL = DSA_TK // LANES
THR_SLOTS = 4
THR_HALF = 64


def _causal_pairs(t, tq, tk):
    qi, kj, last = [], [], []
    for i in range(t // tq):
        nk = (i * tq + tq - 1) // tk + 1
        for j in range(nk):
            qi.append(i), kj.append(j), last.append(int(j == nk - 1))
    return (jnp.asarray(qi, jnp.int32), jnp.asarray(kj, jnp.int32), jnp.asarray(last, jnp.int32))


def _lane_tile(x, width):
    return jnp.concatenate([x] * (width // LANES), axis=1)


def _index_scores(qi, ki, wi, q0, k0):
    tq, tk = qi.shape[0], ki.shape[0]
    s = None
    for h in range(IDX_HEADS):
        sh = jnp.maximum(_dot_t(qi[:, h * IDX_DIM:(h + 1) * IDX_DIM], ki), 0.0)
        sh = sh * _lane_tile(wi[:, h * LANES:(h + 1) * LANES], tk)
        s = sh if s is None else s + sh
    qpos = q0 + lax.broadcasted_iota(jnp.int32, (tq, tk), 0)
    kpos = k0 + lax.broadcasted_iota(jnp.int32, (tq, tk), 1)
    return jnp.where(kpos <= qpos, s, -jnp.inf)


def _stat_update(stats, tile):
    mn, mx, nz = stats
    return (jnp.minimum(mn, jnp.where(tile == -jnp.inf, jnp.inf, tile)), jnp.maximum(mx, tile),
            nz + jnp.where(tile == 0.0, 1.0, 0.0))


def _stat_init(shape):
    return jnp.full(shape, jnp.inf, F32), jnp.full(shape, -jnp.inf, F32), jnp.zeros(shape, F32)


def _stat_rows(stats):
    mn, mx, nz = stats
    return (jnp.min(mn, axis=-1, keepdims=True), jnp.max(mx, axis=-1, keepdims=True),
            jnp.sum(nz, axis=-1, keepdims=True))


def _kth_largest_rows(sc_ref, r0, ncol, kf, n_finite, row_stats):
    n_rows = kf.shape[0]
    half = min(THR_HALF, n_rows)
    rows = pl.ds(r0, n_rows)
    shape = (n_rows, LANES)

    def col_reduce(fn, init, combine):
        def body(c, acc):
            for u in range(THR_UNROLL):
                acc = combine(acc, fn(sc_ref[c * THR_UNROLL + u, rows, :], 1.0))
            return acc
        return lax.fori_loop(0, ncol // THR_UNROLL, body, jnp.full(shape, init, F32))

    def count_where(reduce, pred):
        return jnp.sum(reduce(lambda b, w: jnp.where(pred(b), w, 0.0), 0.0, jnp.add), axis=-1, keepdims=True)

    def count_gt(reduce, m):
        mb = jnp.broadcast_to(m, shape)
        return count_where(reduce, lambda b: b > mb)

    def snap(reduce, lx, hi):
        lb, hb = jnp.broadcast_to(lx, shape), jnp.broadcast_to(hi, shape)
        lo_acc = reduce(lambda b, w: jnp.where(b > lb, b, jnp.inf), jnp.inf, jnp.minimum)
        hi_acc = reduce(lambda b, w: jnp.where(b <= hb, b, -jnp.inf), -jnp.inf, jnp.maximum)
        return jnp.min(lo_acc, axis=-1, keepdims=True), jnp.max(hi_acc, axis=-1, keepdims=True)

    def bisect(reduce, k, bounds):
        lx, hi, a, b, c_lx, c_hi = bounds
        m = 0.5 * a + 0.5 * b
        c = count_gt(reduce, m)
        up = c >= k
        return (jnp.where(up, m, lx), jnp.where(up, hi, m), jnp.where(up, m, a), jnp.where(up, b, m),
                jnp.where(up, c, c_lx), jnp.where(up, c_hi, c))

    def search(reduce, k, lx, hi):
        def round_(state):
            lx, hi, _, _ = state
            d_lo, d_hi = snap(reduce, lx, hi)
            m = 0.5 * d_lo + 0.5 * d_hi
            m = jnp.where(m < d_hi, m, d_lo)
            up = count_gt(reduce, m) >= k
            lx, hi = jnp.where(up, m, lx), jnp.where(up, hi, m)
            bounds = (lx, hi, jnp.maximum(lx, d_lo), jnp.minimum(hi, d_hi), jnp.zeros_like(lx), jnp.zeros_like(lx))
            for _ in range(THR_CHEAP_STEPS):
                bounds = bisect(reduce, k, bounds)
            open_rows = jnp.max(jnp.where(d_lo < d_hi, 1.0, 0.0))
            return bounds[0], bounds[1], d_lo, open_rows
        init = (lx, hi, jnp.zeros((n_rows, 1), F32), jnp.float32(1.0))
        return lax.while_loop(lambda s: s[3] > 0.0, round_, init)[2]

    lx = jnp.full((n_rows, 1), -jnp.inf, F32)
    hi = jnp.full((n_rows, 1), jnp.inf, F32)
    d_lo, d_hi, n_zero = row_stats
    zeros_inside = lambda lo, up: jnp.where(jnp.logical_and(lo < 0.0, up >= 0.0), n_zero, 0.0)

    def warm(state):
        bounds = bisect(col_reduce, kf, state[:6])
        held = bounds[4] - bounds[5] - zeros_inside(bounds[0], bounds[1])
        crowded = jnp.max(held) > float(THR_SLOT_TARGET)
        return (*bounds, state[6] + 1, jnp.logical_and(crowded, state[6] + 1 < THR_WARM_STEPS))
    warm_init = (lx, hi, d_lo, d_hi, n_finite, jnp.zeros((n_rows, 1), F32), jnp.int32(0), jnp.bool_(True))
    lx, hi, _, _, _, c_hi, _, _ = lax.while_loop(lambda st: st[7], warm, warm_init)

    def compact(h0):
        hrows = pl.ds(r0 + h0, half)
        hshape = (half, LANES)
        lb = jnp.broadcast_to(lx[h0:h0 + half], hshape)
        hb = jnp.broadcast_to(hi[h0:h0 + half], hshape)

        def body(c, carry):
            cnt, bufs = carry[0], list(carry[1:])
            for u in range(THR_UNROLL):
                v = sc_ref[c * THR_UNROLL + u, hrows, :]
                inside = jnp.where(v == 0.0, jnp.inf, jnp.where(v > lb, v, jnp.inf)) <= hb
                key = jnp.where(inside, cnt, -1.0)
                bufs = [jnp.where(key == float(slot), v, buf) for slot, buf in enumerate(bufs)]
                cnt = cnt + jnp.where(inside, 1.0, 0.0)
            return (cnt, *bufs)
        empty = jnp.full(hshape, -jnp.inf, F32)
        return lax.fori_loop(0, ncol // THR_UNROLL, body, (jnp.zeros(hshape, F32),) + (empty,) * THR_SLOTS)

    halves = [compact(h0) for h0 in range(0, n_rows, half)]
    cnt = jnp.concatenate([h[0] for h in halves], axis=0)
    bufs = [jnp.concatenate([h[1 + slot] for h in halves], axis=0) for slot in range(THR_SLOTS)]

    z_in = zeros_inside(lx, hi)
    lane0 = lax.broadcasted_iota(jnp.int32, shape, 1) == 0
    zero_tile = jnp.where(jnp.logical_and(lane0, z_in > 0.0), 0.0, -jnp.inf)
    zero_mult = jnp.where(lane0, z_in, 0.0)

    def small_reduce(fn, init, combine):
        acc = combine(jnp.full(shape, init, F32), fn(zero_tile, zero_mult))
        for buf in bufs:
            acc = combine(acc, fn(buf, 1.0))
        return acc

    def from_slots():
        thr = search(small_reduce, kf - c_hi, lx, hi)
        tb = jnp.broadcast_to(thr, shape)
        return (thr, c_hi + count_where(small_reduce, lambda b: b > tb),
                c_hi + count_where(small_reduce, lambda b: b >= tb))

    def from_all():
        thr = search(col_reduce, kf, lx, hi)
        tb = jnp.broadcast_to(thr, shape)
        return thr, count_where(col_reduce, lambda b: b > tb), count_where(col_reduce, lambda b: b >= tb)

    return lax.cond(jnp.max(cnt) > float(THR_SLOTS), from_all, from_slots)


def _tie_cutoff_rows(sc_ref, r0, ncol, thr, need):
    n_rows = thr.shape[0]
    rows = pl.ds(r0, n_rows)
    shape = (n_rows, LANES)
    tb, nb = jnp.broadcast_to(thr, shape), jnp.broadcast_to(need, shape)
    lane = lax.broadcasted_iota(jnp.int32, shape, 1).astype(F32)
    ri = lax.broadcasted_iota(jnp.int32, (LANES, LANES), 0)
    ci = lax.broadcasted_iota(jnp.int32, (LANES, LANES), 1)
    prefix = jnp.where(ri <= ci, 1.0, 0.0).astype(BF16)
    total = jnp.ones((LANES, LANES), BF16)

    def body(c, carry):
        run, best = carry
        eq = sc_ref[c, rows, :] == tb
        hits = jnp.where(eq, 1.0, 0.0).astype(BF16)
        found = jnp.logical_and(eq, run + _dot(hits, prefix) == nb)
        best = jnp.where(found, lane + (c * LANES).astype(F32), best)
        return run + _dot(hits, total), best
    _, best = lax.fori_loop(0, ncol, body, (jnp.zeros(shape, F32), jnp.full(shape, -1.0, F32)))
    return jnp.max(best, axis=-1, keepdims=True).astype(jnp.int32)


def _dsa_thr_kernel(qi_blk, kj_blk, last_blk, qi_ref, wi_ref, ki_ref, thr_ref, cut_ref, sc_ref, stat_ref):
    p = pl.program_id(0)
    i, j = qi_blk[p], kj_blk[p]
    tq, tk = qi_ref.shape[0], ki_ref.shape[0]
    s = _index_scores(qi_ref[...], ki_ref[...], wi_ref[...], i * tq, j * tk)

    @pl.when(j == 0)
    def _():
        for slot, init in enumerate(_stat_init((tq, LANES))):
            stat_ref[slot] = init

    stats = (stat_ref[0], stat_ref[1], stat_ref[2])
    for u in range(tk // LANES):
        tile = s[:, u * LANES:(u + 1) * LANES]
        sc_ref[j * (tk // LANES) + u] = tile
        stats = _stat_update(stats, tile)
    for slot, value in enumerate(stats):
        stat_ref[slot] = value
    n_keys = sc_ref.shape[0] * LANES

    @pl.when(last_blk[p] == 1)
    def _():
        ncol = (j + 1) * (tk // LANES)
        for g in range(tq // THR_ROWS):
            r0 = g * THR_ROWS
            qpos = i * tq + r0 + lax.broadcasted_iota(jnp.int32, (THR_ROWS, 1), 0)
            kf = jnp.minimum(qpos + 1, TOPK_MAX).astype(F32)
            row_stats = _stat_rows(tuple(stat_ref[slot, r0:r0 + THR_ROWS, :] for slot in range(3)))
            thr, n_gt, n_ge = _kth_largest_rows(sc_ref, r0, ncol, kf, (qpos + 1).astype(F32), row_stats)
            thr_ref[r0:r0 + THR_ROWS, :] = jnp.broadcast_to(thr, (THR_ROWS, LANES))
            cut_ref[r0:r0 + THR_ROWS, :] = jnp.full((THR_ROWS, LANES), n_keys, jnp.int32)

            @pl.when(jnp.max(n_ge - kf) > 0.0)
            def _():
                cut = _tie_cutoff_rows(sc_ref, r0, ncol, thr, kf - n_gt)
                cut = jnp.where(n_ge > kf, cut, n_keys)
                cut_ref[r0:r0 + THR_ROWS, :] = jnp.broadcast_to(cut, (THR_ROWS, LANES))


ATTN_ROWS = 64


def _dsa_attn_kernel(qi_blk, kj_blk, last_blk, q_ref, qi_ref, wi_ref, thr_ref, cut_ref, ki_ref, k_ref, vx_ref,
                     o_ref, m_ref, acc_ref, bias_ref, s_ref, p_ref, alpha_ref):
    p = pl.program_id(0)
    i, j = qi_blk[p], kj_blk[p]
    tq, tk = q_ref.shape[1], k_ref.shape[1]

    @pl.when(j == 0)
    def _():
        m_ref[...] = jnp.full(m_ref.shape, NEG_BIG, F32)
        acc_ref[...] = jnp.zeros(acc_ref.shape, F32)

    s = _index_scores(qi_ref[...], ki_ref[...], wi_ref[...], i * tq, j * tk)
    thr = _lane_tile(thr_ref[...], tk)
    kpos = j * tk + lax.broadcasted_iota(jnp.int32, (tq, tk), 1)
    sel = jnp.logical_or(s > thr, jnp.logical_and(s == thr, kpos <= _lane_tile(cut_ref[...], tk)))
    bias_ref[...] = jnp.where(sel, 0.0, NEG_BIG)

    def logits(h):
        s_ref[h % 2] = _dot_t(q_ref[h], k_ref[h])

    def softmax(h):
        slot = h % 2
        for r0 in range(0, tq, ATTN_ROWS):
            rows = slice(r0, r0 + ATTN_ROWS)
            x = s_ref[slot, rows, :] + bias_ref[rows, :]
            m_prev = m_ref[h, rows, :]
            m_new = jnp.maximum(m_prev, jnp.max(x, axis=-1, keepdims=True))
            p_ref[slot, rows, :] = jnp.exp(x - _lane_tile(m_new, tk)).astype(BF16)
            alpha_ref[slot, rows, :] = jnp.exp(m_prev - m_new)
            m_ref[h, rows, :] = m_new

    def values(h):
        acc_ref[h] = alpha_ref[h % 2] * acc_ref[h] + _dot(p_ref[h % 2], vx_ref[h])

    logits(0)
    logits(1)
    softmax(0)

    def head_step(h, carry):
        values(h - 2)
        softmax(h - 1)
        logits(h)
        return carry
    lax.fori_loop(2, A_HEADS, head_step, 0)
    softmax(A_HEADS - 1)
    values(A_HEADS - 2)
    values(A_HEADS - 1)

    @pl.when(last_blk[p] == 1)
    def _():
        o_ref[...] = jnp.concatenate(
            [acc_ref[h][:, 0:HEAD_DIM] / acc_ref[h][:, HEAD_DIM:HEAD_DIM + 1] for h in range(A_HEADS)], axis=1)


def _dsa_prompt(q3, k3, vx3, qi, kiwi, ki):
    t = qi.shape[0]
    tq, tk = min(DSA_TQ, t), min(DSA_TK, t)
    assert t % tq == 0 and t % tk == 0 and tq % THR_ROWS == 0
    pairs = _causal_pairs(t, tq, tk)
    n_pairs = pairs[0].shape[0]
    qrow = lambda w: pl.BlockSpec((tq, w), lambda p, qb, kb, lb: (qb[p], 0))
    krow = lambda w: pl.BlockSpec((tk, w), lambda p, qb, kb, lb: (kb[p], 0))
    thr, cut = pl.pallas_call(
        _dsa_thr_kernel,
        grid_spec=pltpu.PrefetchScalarGridSpec(
            num_scalar_prefetch=3, grid=(n_pairs,),
            in_specs=[qrow(IDX_HEADS * IDX_DIM), qrow(IDX_HEADS * LANES), krow(IDX_DIM)],
            out_specs=(qrow(LANES), qrow(LANES)),
            scratch_shapes=[pltpu.VMEM((t // LANES, tq, LANES), F32), pltpu.VMEM((3, tq, LANES), F32)]),
        out_shape=(jax.ShapeDtypeStruct((t, LANES), F32), jax.ShapeDtypeStruct((t, LANES), jnp.int32)),
        compiler_params=_cparams(("arbitrary",)), name="dsa_prompt_threshold",
    )(*pairs, qi, kiwi, ki)
    qhead = pl.BlockSpec((A_HEADS, tq, HEAD_DIM), lambda p, qb, kb, lb: (0, qb[p], 0))
    khead = lambda w: pl.BlockSpec((A_HEADS, tk, w), lambda p, qb, kb, lb: (0, kb[p], 0))
    return pl.pallas_call(
        _dsa_attn_kernel,
        grid_spec=pltpu.PrefetchScalarGridSpec(
            num_scalar_prefetch=3, grid=(n_pairs,),
            in_specs=[qhead, qrow(IDX_HEADS * IDX_DIM), qrow(IDX_HEADS * LANES), qrow(LANES), qrow(LANES),
                      krow(IDX_DIM), khead(HEAD_DIM), khead(LANES)],
            out_specs=qrow(A_WIDTH),
            scratch_shapes=[pltpu.VMEM((A_HEADS, tq, LANES), F32), pltpu.VMEM((A_HEADS, tq, LANES), F32),
                            pltpu.VMEM((tq, tk), F32), pltpu.VMEM((2, tq, tk), F32),
                            pltpu.VMEM((2, tq, tk), BF16), pltpu.VMEM((2, tq, LANES), F32)]),
        out_shape=jax.ShapeDtypeStruct((t, A_WIDTH), F32),
        compiler_params=_cparams(("arbitrary",)), name="dsa_prompt_attention",
    )(*pairs, q3, qi, kiwi, thr, cut, ki, k3, vx3)


IDX_ROWS = 16


def _dsa_sample_scores_kernel(pt_ref, qi_ref, wi_ref, kin_ref, cache_ref, o_ref, buf_ref, sem_ref):
    b, nb = pl.program_id(0), pl.num_programs(0)
    n_pages = pt_ref.shape[1]
    past = n_pages * PAGE_SIZE

    def issue(bb, slot):
        for p in range(n_pages):
            pltpu.make_async_copy(cache_ref.at[0, pt_ref[bb, p]], buf_ref.at[slot, p], sem_ref.at[slot]).start()

    @pl.when(b == 0)
    def _():
        issue(0, 0)

    @pl.when(b + 1 < nb)
    def _():
        issue(b + 1, (b + 1) % 2)

    slot = b % 2
    pltpu.make_async_copy(buf_ref.at[slot], buf_ref.at[slot], sem_ref.at[slot]).wait()
    qi, wi = qi_ref[0], wi_ref[0]

    def pages(g, carry):
        rows = []
        for u in range(SUBLANES):
            s = jnp.maximum(_dot(qi, buf_ref[slot, g * SUBLANES + u].astype(BF16)), 0.0) * wi
            rows.append(jnp.sum(s, axis=0, keepdims=True))
        o_ref[0, pl.ds(pl.multiple_of(g * SUBLANES, SUBLANES), SUBLANES), :] = jnp.concatenate(rows, axis=0)
        return carry
    lax.fori_loop(0, n_pages // SUBLANES, pages, 0)
    kin = kin_ref[0].astype(F32)
    s_new = jnp.maximum(jnp.sum(qi.astype(F32) * kin, axis=-1, keepdims=True), 0.0) * wi
    s_new = jnp.sum(s_new, axis=0, keepdims=True)
    lane = lax.broadcasted_iota(jnp.int32, (SUBLANES, LANES), 1)
    row = lax.broadcasted_iota(jnp.int32, (SUBLANES, LANES), 0)
    o_ref[0, n_pages:n_pages + SUBLANES, :] = jnp.where(jnp.logical_and(lane == 0, row == 0), s_new, -jnp.inf)


def _dsa_sample_scores(page_table, qi3, wi3, ki_new, cache_idx_t):
    bsz, n_pages = page_table.shape
    assert n_pages % SUBLANES == 0
    n_rows = n_pages + SUBLANES
    blk = lambda a: pl.BlockSpec((1,) + a.shape[1:], lambda b, pt: (b, 0, 0))
    return pl.pallas_call(
        _dsa_sample_scores_kernel,
        grid_spec=pltpu.PrefetchScalarGridSpec(
            num_scalar_prefetch=1, grid=(bsz,),
            in_specs=[blk(qi3), blk(wi3), blk(ki_new), pl.BlockSpec(memory_space=pl.ANY)],
            out_specs=pl.BlockSpec((1, n_rows, LANES), lambda b, pt: (b, 0, 0)),
            scratch_shapes=[pltpu.VMEM((2, n_pages, IDX_DIM, PAGE_SIZE), F32), pltpu.SemaphoreType.DMA((2,))]),
        out_shape=jax.ShapeDtypeStruct((bsz, n_rows, LANES), F32),
        compiler_params=_cparams(("arbitrary",)), name="dsa_sample_scores",
    )(page_table, qi3, wi3, ki_new, cache_idx_t)


def _topk_mask_kernel(k_sel, n_finite, sc_ref, mask_ref):
    n_tiles, bsz, _ = sc_ref.shape
    width = n_tiles * LANES
    ncol = n_tiles + 0 * pl.program_id(0)
    shape = (bsz, LANES)
    kf = jnp.full((bsz, 1), float(k_sel), F32)
    row_stats = _stat_rows(lax.fori_loop(0, ncol, lambda c, st: _stat_update(st, sc_ref[c]), _stat_init(shape)))
    thr, n_gt, n_ge = _kth_largest_rows(sc_ref, 0, ncol, kf, jnp.full((bsz, 1), float(n_finite), F32), row_stats)
    cut = lax.cond(jnp.max(n_ge - kf) > 0.0,
                   lambda: jnp.where(n_ge > kf, _tie_cutoff_rows(sc_ref, 0, ncol, thr, kf - n_gt), width),
                   lambda: jnp.full((bsz, 1), width, jnp.int32))
    tb, cb = jnp.broadcast_to(thr, shape), jnp.broadcast_to(cut, shape)
    lane = lax.broadcasted_iota(jnp.int32, shape, 1)

    def mark(c, carry):
        v = sc_ref[c]
        sel = jnp.logical_or(v > tb, jnp.logical_and(v == tb, lane + c * LANES <= cb))
        mask_ref[c] = jnp.where(sel, 1.0, 0.0)
        return carry
    lax.fori_loop(0, ncol, mark, 0)


def _topk_mask(scores_t, k_sel, n_finite):
    assert (scores_t.shape[0] * LANES) % DSA_TK == 0 and n_finite >= k_sel
    whole = pl.BlockSpec(scores_t.shape, lambda i: (0, 0, 0))
    return pl.pallas_call(
        functools.partial(_topk_mask_kernel, k_sel, n_finite), grid=(1,),
        in_specs=[whole], out_specs=whole,
        out_shape=jax.ShapeDtypeStruct(scores_t.shape, F32),
        compiler_params=_cparams(("arbitrary",)), name="dsa_sample_topk",
    )(scores_t)


SAMPLE_PAGES_PER_STEP = 16


def _dsa_sample_attn_kernel(pt_ref, qt_ref, knt_ref, vnt_ref, mask_ref, ck_ref, cv_ref, o_ref,
                            kbuf, vbuf, sem_ref, m_ref, l_ref, acc_ref):
    b, c = pl.program_id(0), pl.program_id(1)
    nb, nc = pl.num_programs(0), pl.num_programs(1)
    pps = SAMPLE_PAGES_PER_STEP
    n_pages = pt_ref.shape[1]
    step = b * nc + c

    def issue(s, slot):
        bb, cc = s // nc, s % nc
        for i in range(pps):
            page = pt_ref[bb, cc * pps + i]
            pltpu.make_async_copy(ck_ref.at[0, page], kbuf.at[slot, i], sem_ref.at[0, slot]).start()
            pltpu.make_async_copy(cv_ref.at[0, page], vbuf.at[slot, i], sem_ref.at[1, slot]).start()

    @pl.when(step == 0)
    def _():
        issue(0, 0)

    @pl.when(step + 1 < nb * nc)
    def _():
        issue(step + 1, (step + 1) % 2)

    slot = step % 2
    pltpu.make_async_copy(kbuf.at[slot], kbuf.at[slot], sem_ref.at[0, slot]).wait()
    pltpu.make_async_copy(vbuf.at[slot], vbuf.at[slot], sem_ref.at[1, slot]).wait()

    @pl.when(c == 0)
    def _():
        m_ref[...] = jnp.full(m_ref.shape, NEG_BIG, F32)
        l_ref[...] = jnp.zeros(l_ref.shape, F32)
        acc_ref[...] = jnp.zeros(acc_ref.shape, F32)

    keep = mask_ref[0, pl.ds(pl.multiple_of(c * pps, pps), pps), :] > 0.0
    qt = qt_ref[0].astype(F32)
    heads = range(A_HEADS)
    qcols = [jnp.broadcast_to(qt[:, h:h + 1], (HEAD_DIM, PAGE_SIZE)) for h in heads]
    lg = [jnp.concatenate([jnp.sum(kbuf[slot, i, h] * qcols[h], axis=0, keepdims=True) for i in range(pps)], axis=0)
          for h in heads]
    lg = [jnp.where(keep, x, NEG_BIG) for x in lg]
    m_old = [m_ref[h] for h in heads]
    m_new = [jnp.maximum(m_old[h], jnp.max(lg[h], axis=(0, 1), keepdims=True)) for h in heads]
    alpha = [jnp.exp(m_old[h] - m_new[h]) for h in heads]
    p = [jnp.where(keep, jnp.exp(lg[h] - m_new[h][0:1, :]), 0.0) for h in heads]
    for h in heads:
        l_ref[h] = alpha[h] * l_ref[h] + jnp.sum(p[h], axis=(0, 1), keepdims=True)
        m_ref[h] = m_new[h]
    for h in heads:
        pv = vbuf[slot, 0, h] * p[h][0:1, :]
        for i in range(1, pps):
            pv = pv + vbuf[slot, i, h] * p[h][i:i + 1, :]
        acc_ref[h] = alpha[h][0:1, :] * acc_ref[h] + pv

    @pl.when(c == nc - 1)
    def _():
        lane = lax.broadcasted_iota(jnp.int32, (HEAD_DIM, LANES), 1)
        keep_new = mask_ref[0, n_pages:n_pages + 1, 0:1] > 0.0
        out = jnp.zeros((HEAD_DIM, LANES), F32)
        for h in range(A_HEADS):
            lg_new = jnp.sum(knt_ref[0][:, h:h + 1] * qt[:, h:h + 1], axis=0, keepdims=True)
            lg_new = jnp.where(keep_new, lg_new, NEG_BIG)
            m_old = m_ref[h][0:1, 0:1]
            m_new = jnp.maximum(m_old, lg_new)
            alpha = jnp.exp(m_old - m_new)
            p_new = jnp.where(keep_new, jnp.exp(lg_new - m_new), 0.0)
            l_fin = alpha * l_ref[h][0:1, 0:1] + p_new
            col = alpha * jnp.sum(acc_ref[h], axis=-1, keepdims=True) + p_new * vnt_ref[0][:, h:h + 1]
            out = jnp.where(lane == h, col / l_fin, out)
        o_ref[0] = out


def _dsa_sample_attn(page_table, q_t, k_new_t, v_new_t, mask3, cache_k_t, cache_v_t):
    bsz, n_pages = page_table.shape
    pps = SAMPLE_PAGES_PER_STEP
    assert n_pages % pps == 0
    col = lambda: pl.BlockSpec((1, HEAD_DIM, A_HEADS), lambda b, c, pt: (b, 0, 0))
    page_buf = pltpu.VMEM((2, pps, A_HEADS, HEAD_DIM, PAGE_SIZE), F32)
    return pl.pallas_call(
        _dsa_sample_attn_kernel,
        grid_spec=pltpu.PrefetchScalarGridSpec(
            num_scalar_prefetch=1, grid=(bsz, n_pages // pps),
            in_specs=[col(), col(), col(), pl.BlockSpec((1,) + mask3.shape[1:], lambda b, c, pt: (b, 0, 0)),
                      pl.BlockSpec(memory_space=pl.ANY), pl.BlockSpec(memory_space=pl.ANY)],
            out_specs=pl.BlockSpec((1, HEAD_DIM, LANES), lambda b, c, pt: (b, 0, 0)),
            scratch_shapes=[page_buf, page_buf, pltpu.SemaphoreType.DMA((2, 2)),
                            pltpu.VMEM((A_HEADS, SUBLANES, LANES), F32), pltpu.VMEM((A_HEADS, SUBLANES, LANES), F32),
                            pltpu.VMEM((A_HEADS, HEAD_DIM, LANES), F32)]),
        out_shape=jax.ShapeDtypeStruct((bsz, HEAD_DIM, LANES), F32),
        compiler_params=_cparams(("arbitrary", "arbitrary")), name="dsa_sample_attention",
    )(page_table, q_t, k_new_t, v_new_t, mask3, cache_k_t, cache_v_t)


def _head_ones():
    h = np.arange(B_WIDTH) // B_HEAD_DIM
    return jnp.asarray((h[:, None] == h[None, :]).astype(np.float32))


def _rwkv_pre_kernel(sh_ref, w0_ref, wup_ref, a0_ref, aup_ref, gup_ref, kk_ref_, ka_ref, g1_ref,
                     ld_ref, kk_ref, kb_ref, kp_ref, g_ref):
    k = sh_ref[:, B_WIDTH:2 * B_WIDTH]
    o = 3 * B_WIDTH
    xw = sh_ref[:, o:o + DECAY_LORA]
    xa = sh_ref[:, o + DECAY_LORA:o + DECAY_LORA + AAA_LORA]
    xg = sh_ref[:, o + DECAY_LORA + AAA_LORA:SHIFT_W]
    w = w0_ref[...] + _dot3(jnp.tanh(xw), wup_ref[...])
    logw = -jax.nn.softplus(-w) - 0.5
    ld_ref[...] = -jnp.exp(logw)
    a = jax.nn.sigmoid(a0_ref[...] + _dot3(xa, aup_ref[...]))
    g_ref[...] = _dot3(jax.nn.sigmoid(xg), gup_ref[...])
    kk = k * kk_ref_[...]
    kk = kk / jnp.maximum(jnp.sqrt(_dot3(kk * kk, g1_ref[...])), 1e-12)
    kk_ref[...] = kk
    kb_ref[...] = kk * a
    kp_ref[...] = k * (1.0 + (a - 1.0) * ka_ref[...])


def _rwkv_pre(shifted, w0, w_lora_up, a0, a_lora_up, g_lora_up, k_k, k_a, tm):
    t = shifted.shape[0]
    row = lambda w: pl.BlockSpec((tm, w), lambda i: (i, 0))
    const = lambda a: pl.BlockSpec(a.shape, lambda i: (0, 0))
    g1 = _head_ones()
    args = (shifted, w0, w_lora_up, a0, a_lora_up, g_lora_up, k_k, k_a, g1)
    out = jax.ShapeDtypeStruct((t, B_WIDTH), F32)
    return pl.pallas_call(
        _rwkv_pre_kernel, grid=(t // tm,),
        in_specs=[row(SHIFT_W)] + [const(a) for a in args[1:]],
        out_specs=(row(B_WIDTH),) * 5, out_shape=(out,) * 5,
        compiler_params=_cparams(("arbitrary",)), name="rwkv_prep",
    )(*args)


def _rwkv_seq_kernel(r_ref, v_ref, ld_ref, kk_ref, kb_ref, kp_ref, y_ref, so_ref, st_ref):
    c = CHUNK

    @pl.when(pl.program_id(0) == 0)
    def _():
        st_ref[...] = jnp.zeros(st_ref.shape, F32)

    ri = lax.broadcasted_iota(jnp.int32, (c, c), 0)
    ci = lax.broadcasted_iota(jnp.int32, (c, c), 1)
    incl, strict, eye = ri >= ci, ri > ci, ri == ci
    ld = ld_ref[...]
    tri = jnp.where(incl, 1.0, 0.0).astype(BF16)
    ld_hi = ld.astype(BF16)
    rest = ld - ld_hi.astype(F32)
    ld_mid = rest.astype(BF16)
    ld_lo = (rest - ld_mid.astype(F32)).astype(BF16)
    cum = _dot(tri, ld_hi) + (_dot(tri, ld_mid) + _dot(tri, ld_lo))
    cum_c = cum[c - 1:c, :]
    e_neg, e_rem = jnp.exp(-cum), jnp.exp(cum_c - cum)
    kk, kb, kp = kk_ref[...], kb_ref[...], kp_ref[...]
    a_all = -kk * jnp.exp(cum - ld)
    r_all = r_ref[...] * jnp.exp(cum)
    bt_all, kt_all = kb * e_neg, kp * e_neg
    bh_all, kh_all = kb * e_rem, kp * e_rem
    pc = jnp.exp(cum_c)
    v_all = v_ref[...]
    n = B_HEAD_DIM
    heads = range(B_HEADS)
    hsl = [slice(h * n, (h + 1) * n) for h in heads]
    p = [_dot3_t(jnp.concatenate([a_all[:, s], r_all[:, s]], 0), jnp.concatenate([bt_all[:, s], kt_all[:, s]], 0))
         for s in hsl]
    l_ak = [jnp.where(strict, q[:c, c:], 0.0) for q in p]
    m_rb = [jnp.where(incl, q[c:, :c], 0.0) for q in p]
    m_rk = [jnp.where(incl, q[c:, c:], 0.0) for q in p]
    lp = [jnp.where(strict, q[:c, :c], 0.0) for q in p]
    x = [jnp.concatenate([a_all[:, s], _dot3(l_ak[h], v_all[:, s])], 1) for h, s in enumerate(hsl)]
    n_dbl = int(np.log2(c))
    for it in range(n_dbl):
        x = [x[h] + _dot3(lp[h], x[h]) for h in heads]
        if it + 1 < n_dbl:
            lp = [_dot3(lp[h], lp[h]) for h in heads]
    bm = [_dot3(jnp.concatenate([bh_all[:, s].T, m_rb[h]], 0), x[h]) for h, s in enumerate(hsl)]
    kv = [_dot3(jnp.concatenate([kh_all[:, s].T, m_rk[h]], 0), v_all[:, s]) for h, s in enumerate(hsl)]
    for h, s in enumerate(hsl):
        m_c = jnp.where(eye, jnp.broadcast_to(pc[:, s], (n, n)), 0.0) + bm[h][:n, :n]
        g_c = r_all[:, s] + bm[h][n:, :n]
        gs = _dot3(jnp.concatenate([g_c, m_c], 0), st_ref[h])
        y_ref[:, s] = gs[:c, :] + (bm[h][n:, n:] + kv[h][n:, :])
        st_ref[h] = gs[c:, :] + (bm[h][:n, n:] + kv[h][:n, :])
    so_ref[...] = st_ref[...]


def _rwkv_seq(shifted, ld, kk, kb, kp):
    t = shifted.shape[0]
    assert t % CHUNK == 0
    col = lambda j: pl.BlockSpec((CHUNK, B_WIDTH), lambda i: (i, j))
    state = jax.ShapeDtypeStruct((B_HEADS, B_HEAD_DIM, B_HEAD_DIM), F32)
    return pl.pallas_call(
        _rwkv_seq_kernel, grid=(t // CHUNK,),
        in_specs=[col(0), col(2), col(0), col(0), col(0), col(0)],
        out_specs=(col(0), pl.BlockSpec(state.shape, lambda i: (0, 0, 0))),
        out_shape=(jax.ShapeDtypeStruct((t, B_WIDTH), F32), state),
        scratch_shapes=[pltpu.VMEM(state.shape, F32)],
        compiler_params=_cparams(("arbitrary",)), name="rwkv_chunked_scan",
    )(shifted, shifted, ld, kk, kb, kp)


RWKV_STEP_ROWS = 8


def _rwkv_step_kernel(s_ref, r_ref, ld_ref, kk_ref, kb_ref, kp_ref, v_ref, so_ref, y_ref):
    def one(b, carry):
        row = pl.ds(b, 1)
        r, w, kk, kb, kp = r_ref[row, :], jnp.exp(ld_ref[row, :]), kk_ref[row, :], kb_ref[row, :], kp_ref[row, :]
        for h in range(B_HEADS):
            hs = slice(h * B_HEAD_DIM, (h + 1) * B_HEAD_DIM)
            s = s_ref[b, h]
            sa = -jnp.sum(s * kk[:, hs], axis=-1, keepdims=True)
            s2 = s * w[:, hs] + sa * kb[:, hs] + v_ref[b, h] * kp[:, hs]
            so_ref[b, h] = s2
            y_ref[b, h] = jnp.sum(s2 * r[:, hs], axis=-1, keepdims=True)
        return carry
    lax.fori_loop(0, s_ref.shape[0], one, 0)


def _rwkv_step(state, shifted, ld, kk, kb, kp, v_col):
    bsz = state.shape[0]
    nb = RWKV_STEP_ROWS
    assert bsz % nb == 0
    row = lambda: pl.BlockSpec((nb, B_WIDTH), lambda i: (i, 0))
    st = pl.BlockSpec((nb, B_HEADS, B_HEAD_DIM, B_HEAD_DIM), lambda i: (i, 0, 0, 0))
    colv = pl.BlockSpec((nb, B_HEADS, B_HEAD_DIM, 1), lambda i: (i, 0, 0, 0))
    return pl.pallas_call(
        _rwkv_step_kernel, grid=(bsz // nb,),
        in_specs=[st, row(), row(), row(), row(), row(), colv],
        out_specs=(st, colv),
        out_shape=(jax.ShapeDtypeStruct(state.shape, F32),
                   jax.ShapeDtypeStruct((bsz, B_HEADS, B_HEAD_DIM, 1), F32)),
        compiler_params=_cparams(("arbitrary",)), name="rwkv_single_step",
    )(state, shifted, ld, kk, kb, kp, v_col)


def _rwkv_post_kernel(y_ref, r_ref, v_ref, kp_ref, g_ref, rk_ref, gng_ref, gnb_ref, g1_ref, o_ref):
    g1 = g1_ref[...]
    inv_n = 1.0 / B_HEAD_DIM
    y = y_ref[...]
    d = y - _dot3(y, g1) * inv_n
    var = _dot3(d * d, g1) * inv_n
    yn = d * lax.rsqrt(var + GN_EPS) * gng_ref[...] + gnb_ref[...]
    bonus = _dot3(r_ref[...] * kp_ref[...] * rk_ref[...], g1) * v_ref[...]
    o_ref[...] = (yn + bonus) * g_ref[...]


def _rwkv_post(y, shifted, kp, g, r_k, gn_g, gn_b, tm):
    t = y.shape[0]
    col = lambda j: pl.BlockSpec((tm, B_WIDTH), lambda i: (i, j))
    const = lambda a: pl.BlockSpec(a.shape, lambda i: (0, 0))
    g1 = _head_ones()
    return pl.pallas_call(
        _rwkv_post_kernel, grid=(t // tm,),
        in_specs=[col(0), col(0), col(2), col(0), col(0), const(r_k), const(gn_g), const(gn_b), const(g1)],
        out_specs=col(0), out_shape=jax.ShapeDtypeStruct((t, B_WIDTH), F32),
        compiler_params=_cparams(("arbitrary",)), name="rwkv_output_norm",
    )(y, shifted, shifted, kp, g, r_k, gn_g, gn_b, g1)


def _outproj_kernel(x_ref, attn_ref, rw_ref, gate_ref, wa_ref, wb_ref, wo_ref, nf_ref, wr_ref, br_ref,
                    h_ref, hn_ref, comb_ref):
    ga, gb = gate_ref[:, :D_MODEL], gate_ref[:, D_MODEL:]
    merged = ga * _dot(attn_ref[...].astype(BF16), wa_ref[...]) + gb * _dot(rw_ref[...].astype(BF16), wb_ref[...])
    h = x_ref[...] + _dot(merged.astype(BF16), wo_ref[...])
    h_ref[...] = h
    hn = _rms(h, nf_ref[...])
    hn_ref[...] = hn.astype(BF16)
    logits = _dotx(hn, wr_ref[...]) + br_ref[...]
    lane = lax.broadcasted_iota(jnp.int32, logits.shape, 1)
    big = jnp.int32(LANES)
    first = lambda hit: jnp.min(jnp.where(hit, lane, big), axis=-1, keepdims=True)
    is_g = lane < N_GROUPS
    gl = jnp.where(is_g, logits, -jnp.inf)
    gmax = jnp.max(gl, axis=-1, keepdims=True)
    p_group = 1.0 / jnp.sum(jnp.exp(gl - gmax), axis=-1, keepdims=True)
    g_sel = first(gl == gmax)
    e0 = N_GROUPS + g_sel * EXPERTS_PER_GROUP
    el = jnp.where(jnp.logical_and(lane >= e0, lane < e0 + EXPERTS_PER_GROUP), logits, -jnp.inf)
    v1 = jnp.max(el, axis=-1, keepdims=True)
    i1 = first(el == v1)
    el2 = jnp.where(lane == i1, -jnp.inf, el)
    v2 = jnp.max(el2, axis=-1, keepdims=True)
    i2 = first(el2 == v2)
    e21 = jnp.exp(v2 - v1)
    w1 = p_group / (1.0 + e21)
    w2 = p_group * e21 / (1.0 + e21)
    comb_ref[...] = jnp.where(lane == i1 - N_GROUPS, w1, 0.0) + jnp.where(lane == i2 - N_GROUPS, w2, 0.0)


def _outproj(x, attn, rw, gates, wa, wb, wo, norm_ffn, w_router, b_router, tm):
    t = x.shape[0]
    row = lambda w: pl.BlockSpec((tm, w), lambda i: (i, 0))
    const = lambda a: pl.BlockSpec(a.shape, lambda i: (0, 0))
    return pl.pallas_call(
        _outproj_kernel, grid=(t // tm,),
        in_specs=[row(D_MODEL), row(A_WIDTH), row(B_WIDTH), row(2 * D_MODEL), const(wa), const(wb), const(wo),
                  const(norm_ffn), const(w_router), const(b_router)],
        out_specs=(row(D_MODEL), row(D_MODEL), row(LANES)),
        out_shape=(jax.ShapeDtypeStruct((t, D_MODEL), F32), jax.ShapeDtypeStruct((t, D_MODEL), BF16),
                   jax.ShapeDtypeStruct((t, LANES), F32)),
        compiler_params=_cparams(("arbitrary",)), name="outproj_router",
    )(x, attn, rw, gates, wa, wb, wo, norm_ffn, w_router, b_router)


def _moe_kernel(h_ref, hn_ref, comb_ref, wg_ref, wu_ref, wd_ref, nfin_ref, o_ref, acc_ref):
    e = pl.program_id(1)
    hn = hn_ref[...]
    act = jax.nn.silu(_dot(hn, wg_ref[0])) * _dot(hn, wu_ref[0])
    y = _dot(act.astype(BF16), wd_ref[0])
    lane = lax.broadcasted_iota(jnp.int32, comb_ref.shape, 1)
    c_e = jnp.sum(jnp.where(lane == e, comb_ref[...], 0.0), axis=-1, keepdims=True)

    @pl.when(e == 0)
    def _():
        acc_ref[...] = c_e * y

    @pl.when(e > 0)
    def _():
        acc_ref[...] += c_e * y

    @pl.when(e == pl.num_programs(1) - 1)
    def _():
        o_ref[...] = _rms(h_ref[...] + acc_ref[...], nfin_ref[...])


def _moe(h, hn, comb, wg, wu, wd, norm_final, tm):
    t = h.shape[0]
    row = lambda w: pl.BlockSpec((tm, w), lambda i, e: (i, 0))
    return pl.pallas_call(
        _moe_kernel, grid=(t // tm, N_EXPERTS),
        in_specs=[row(D_MODEL), row(D_MODEL), row(LANES),
                  pl.BlockSpec((1, D_MODEL, D_EXPERT), lambda i, e: (e, 0, 0)),
                  pl.BlockSpec((1, D_MODEL, D_EXPERT), lambda i, e: (e, 0, 0)),
                  pl.BlockSpec((1, D_EXPERT, D_MODEL), lambda i, e: (e, 0, 0)),
                  pl.BlockSpec(norm_final.shape, lambda i, e: (0, 0))],
        out_specs=row(D_MODEL), out_shape=jax.ShapeDtypeStruct((t, D_MODEL), F32),
        scratch_shapes=[pltpu.VMEM((tm, D_MODEL), F32)],
        compiler_params=_cparams(("arbitrary", "arbitrary")), name="moe_experts",
    )(h, hn, comb, wg, wu, wd, norm_final)


def _tile(t, pref):
    return min(pref, t)


def _mixer_tail(x, attn, shifted, gates, state_step, weights, ffn):
    (w0, w_lora_up, a0, a_lora_up, g_lora_up, k_k, k_a, r_k, gn_g, gn_b, wa, wb, wo) = weights
    (norm_ffn, w_router, b_router, wg, wu, wd, norm_final) = ffn
    t = x.shape[0]
    tm = _tile(t, 256)
    ld, kk, kb, kp, g = _rwkv_pre(shifted, w0, w_lora_up, a0, a_lora_up, g_lora_up, k_k, k_a, tm)
    if state_step is None:
        y, st = _rwkv_seq(shifted, ld, kk, kb, kp)
        wkv = jnp.swapaxes(st, 1, 2)[None]
    else:
        v_col = shifted[:, 2 * B_WIDTH:3 * B_WIDTH].reshape(t, B_HEADS, B_HEAD_DIM, 1)
        wkv, y_col = _rwkv_step(state_step, shifted, ld, kk, kb, kp, v_col)
        y = y_col.reshape(t, B_WIDTH)
    rw = _rwkv_post(y, shifted, kp, g, r_k, gn_g, gn_b, tm)
    h, hn, comb = _outproj(x, attn, rw, gates, wa, wb, wo, norm_ffn, w_router, b_router, tm)
    out = _moe(h, hn, comb, wg, wu, wd, norm_final, _tile(t, 1024))
    return out, wkv


def kernel(x_prompt, x_sample, cache_k, cache_v, cache_idx_k, state_wkv, state_shift, page_table, norm_mix, w_in, mu_shift, w0, w_lora_up, a0, a_lora_up, g_lora_up, k_k, k_a, r_k, gn_g, gn_b, w_branch_a, w_branch_b, w_out, norm_ffn, w_router_group, b_router_group, w_router_expert, b_router_expert, w_gate, w_up, w_down, norm_final):
    assert w_in.shape[0] == 1, "single-layer kernel"
    bp, tp, _ = x_prompt.shape
    bs, ts, _ = x_sample.shape
    assert bp == 1 and ts == 1
    n_pages = page_table.shape[1]
    past = n_pages * PAGE_SIZE
    row2 = lambda a: a.reshape(1, -1)

    w_packed = _pack_w_in(w_in[0])
    mix_w = (row2(w0[0]), w_lora_up[0], row2(a0[0]), a_lora_up[0], g_lora_up[0], row2(k_k[0]), row2(k_a[0]),
             row2(r_k[0]), row2(gn_g[0]), row2(gn_b[0]),
             w_branch_a[0].astype(BF16), w_branch_b[0].astype(BF16), w_out[0].astype(BF16))
    pad_r = jnp.zeros((D_MODEL, LANES - N_GROUPS - N_EXPERTS), F32)
    w_router = jnp.concatenate([w_router_group[0], w_router_expert[0], pad_r], 1)
    b_router = jnp.concatenate([b_router_group[0], b_router_expert[0], pad_r[0]])[None]
    ffn_w = (row2(norm_ffn[0]), w_router, b_router, w_gate[0].astype(BF16), w_up[0].astype(BF16),
             w_down[0].astype(BF16), row2(norm_final))
    g_mix, mu = row2(norm_mix[0]), row2(mu_shift[0])

    xp = x_prompt[0]
    tabs = _rope_tables(jnp.arange(tp))
    (q3, kf, k3, vf, vx3, qi, kiwi, ki16, wrep, shifted, shift_last, gates) = _inproj(
        xp, g_mix, w_packed, *tabs, mu, jnp.zeros((1, SHIFT_W), F32), True, _tile(tp, 256))
    attn = _dsa_prompt(q3, k3, vx3, qi, wrep, ki16)
    y_p, wkv_p = _mixer_tail(xp, attn, shifted, gates, None, mix_w, ffn_w)

    xs = x_sample[:, 0]
    tabs_s = _rope_tables(jnp.full((bs,), past, jnp.int32))
    (q3_s, kf_s, _, vf_s, _, qi_s, kiwi_s, ki16_s, _, shifted_s, shift_raw_s, gates_s) = _inproj(
        xs, g_mix, w_packed, *tabs_s, mu, state_shift[0], False, _tile(bs, 256))
    pad_h = IDX_ROWS - IDX_HEADS
    qi3 = jnp.pad(qi_s.reshape(bs, IDX_HEADS, IDX_DIM), ((0, 0), (0, pad_h), (0, 0)))
    wi3 = jnp.pad(kiwi_s[:, IDX_DIM:IDX_DIM + IDX_HEADS], ((0, 0), (0, pad_h)))[:, :, None]
    scores = _dsa_sample_scores(page_table, qi3, wi3, ki16_s[:, None, :], jnp.swapaxes(cache_idx_k, 2, 3))
    k_sel = min(TOPK_MAX, (past + ts) // 4)
    mask = jnp.swapaxes(_topk_mask(jnp.swapaxes(scores, 0, 1), k_sel, past + ts), 0, 1)
    cols = lambda a: jnp.swapaxes(a.reshape(bs, A_HEADS, HEAD_DIM), 1, 2)
    page_t = lambda cache: jnp.transpose(cache, (0, 1, 3, 4, 2))
    q_cols = jnp.transpose(q3_s, (1, 2, 0))
    attn_t = _dsa_sample_attn(page_table, q_cols, cols(kf_s), cols(vf_s), mask, page_t(cache_k), page_t(cache_v))
    attn_s = jnp.swapaxes(attn_t[:, :, :A_HEADS], 1, 2).reshape(bs, A_WIDTH)
    y_s, wkv_s = _mixer_tail(xs, attn_s, shifted_s, gates_s, state_wkv[0], mix_w, ffn_w)

    kv5 = lambda a, b, t: a.reshape(1, b, t, A_HEADS, HEAD_DIM)
    return (y_p[None], y_s[:, None],
            kv5(kf, 1, tp), kv5(vf, 1, tp), kiwi[:, :IDX_DIM].reshape(1, 1, tp, IDX_DIM),
            wkv_p[None], shift_last.reshape(1, 1, SHIFT_W),
            kv5(kf_s, bs, 1), kv5(vf_s, bs, 1), kiwi_s[:, :IDX_DIM].reshape(1, bs, 1, IDX_DIM),
            wkv_s[None], shift_raw_s[None])
```

```python
import functools

import jax
import jax.numpy as jnp
import numpy as np
from jax import lax
from jax.experimental import pallas as pl
from jax.experimental.pallas import tpu as pltpu

F32 = jnp.float32
BF16 = jnp.bfloat16

D_MODEL = 1024
PAGE_SIZE = 128
A_HEADS = 8
HEAD_DIM = 64
A_WIDTH = A_HEADS * HEAD_DIM
IDX_HEADS = 4
IDX_DIM = 64
TOPK_MAX = 256
ROT_DIM = HEAD_DIM // 4
ROPE_THETA = 500000.0
B_HEADS = 8
B_HEAD_DIM = 64
B_WIDTH = B_HEADS * B_HEAD_DIM
DECAY_LORA = 64
AAA_LORA = 64
GATE_LORA = 128
GN_EPS = B_HEAD_DIM * 1e-5
SHIFT_W = 3 * B_WIDTH + DECAY_LORA + AAA_LORA + GATE_LORA
N_GROUPS = 4
EXPERTS_PER_GROUP = 8
N_EXPERTS = N_GROUPS * EXPERTS_PER_GROUP
D_EXPERT = 512
RMS_EPS = 1e-6

LANES = 128
SUBLANES = 8
VMEM_LIMIT_BYTES = 56 * 1024 * 1024

QKV_END = 3 * A_WIDTH
QI_END = QKV_END + IDX_HEADS * IDX_DIM
KIWI_END = QI_END + LANES
SHIFT_END = KIWI_END + SHIFT_W
D_IN_PACKED = SHIFT_END + 2 * D_MODEL
IDX_W_SCALE = IDX_HEADS ** -0.5 * IDX_DIM ** -0.5
NEG_BIG = -1e30
LOG2_E = float(np.log2(np.e))
CHUNK = 64


def _cparams(sem):
    return pltpu.CompilerParams(dimension_semantics=sem, vmem_limit_bytes=VMEM_LIMIT_BYTES)


def _dot(a, b):
    return jnp.dot(a, b, preferred_element_type=F32)


def _dot_t(a, b):
    return lax.dot_general(a, b, (((1,), (1,)), ((), ())), preferred_element_type=F32)


def _dotx(a, b):
    return jnp.dot(a, b, preferred_element_type=F32, precision=lax.Precision.HIGHEST)


def _split2(a):
    hi = a.astype(BF16)
    return hi, (a - hi.astype(F32)).astype(BF16)


def _dot3(a, b):
    ah, al = _split2(a)
    bh, bl = _split2(b)
    return _dot(ah, bh) + (_dot(ah, bl) + _dot(al, bh))


def _dot3_t(a, b):
    ah, al = _split2(a)
    bh, bl = _split2(b)
    return _dot_t(ah, bh) + (_dot_t(ah, bl) + _dot_t(al, bh))


def _rms(x, g):
    return x * lax.rsqrt(jnp.mean(x * x, axis=-1, keepdims=True) + RMS_EPS) * g


def _rope_slab(x, c, sa, sb):
    return x * c + pltpu.roll(x, ROT_DIM // 2, 1) * sa + pltpu.roll(x, LANES - ROT_DIM // 2, 1) * sb


def _inproj_kernel(seq_mode, x_ref, g_ref, w_ref, c_ref, sa_ref, sb_ref, mu_ref, prev_ref,
                   q_ref, kf_ref, kb_ref, vf_ref, vb_ref, qi_ref, kiwi_ref, kib_ref, wrep_ref, sh_ref, last_ref,
                   gate_ref, carry_ref):
    tm = x_ref.shape[0]
    xn = _rms(x_ref[...], g_ref[...]).astype(BF16)
    c, sa, sb = c_ref[...], sa_ref[...], sb_ref[...]

    def proj(c0, c1):
        return _dot(xn, w_ref[:, c0:c1])

    def rope(z):
        return jnp.concatenate(
            [_rope_slab(z[:, s:s + LANES], c, sa, sb) for s in range(0, z.shape[1], LANES)], axis=1)

    q = rope(proj(0, A_WIDTH)) * (HEAD_DIM ** -0.5 * LOG2_E)
    k = rope(proj(A_WIDTH, 2 * A_WIDTH))
    kf_ref[...] = k
    v = proj(2 * A_WIDTH, QKV_END)
    vf_ref[...] = v
    lane_h = lax.broadcasted_iota(jnp.int32, (tm, LANES - HEAD_DIM), 1)
    ones_pad = jnp.where(lane_h == 0, 1.0, 0.0)
    for h in range(A_HEADS):
        hs = slice(h * HEAD_DIM, (h + 1) * HEAD_DIM)
        q_ref[h] = q[:, hs].astype(BF16)
        kb_ref[h] = k[:, hs].astype(BF16)
        vb_ref[h] = jnp.concatenate([v[:, hs], ones_pad], axis=1).astype(BF16)
    qi_ref[...] = rope(proj(QKV_END, QI_END)).astype(BF16)
    kiwi = proj(QI_END, KIWI_END)
    lane = lax.broadcasted_iota(jnp.int32, kiwi.shape, 1)
    kiwi = jnp.where(lane < IDX_DIM, _rope_slab(kiwi, c, sa, sb), kiwi * IDX_W_SCALE)
    kiwi_ref[...] = kiwi
    kib_ref[...] = kiwi[:, :IDX_DIM].astype(BF16)
    wrep_ref[...] = jnp.concatenate(
        [jnp.broadcast_to(kiwi[:, IDX_DIM + h:IDX_DIM + h + 1], (tm, LANES)) for h in range(IDX_HEADS)], axis=1)

    u = proj(KIWI_END, SHIFT_END)
    if seq_mode:
        @pl.when(pl.program_id(0) == 0)
        def _():
            carry_ref[...] = prev_ref[...]
        row = lax.broadcasted_iota(jnp.int32, u.shape, 0)
        u_prev = jnp.where(row == 0, carry_ref[...], pltpu.roll(u, 1, 0))
        carry_ref[...] = u[tm - 1:tm, :]
    else:
        u_prev = prev_ref[...]
    sh_ref[...] = u + (u_prev - u) * mu_ref[...]
    last_ref[...] = u[tm - 1:tm, :] if seq_mode else u
    gate_ref[...] = jax.nn.sigmoid(proj(SHIFT_END, D_IN_PACKED))


def _inproj(x, norm_g, w_packed, rope_c, rope_sa, rope_sb, mu, prev, seq_mode, tm):
    t = x.shape[0]
    assert t % tm == 0
    row = lambda w: pl.BlockSpec((tm, w), lambda i: (i, 0))
    const = lambda a: pl.BlockSpec(a.shape, lambda i: (0, 0))
    prev_spec = const(prev) if seq_mode else row(SHIFT_W)
    head = lambda w: pl.BlockSpec((A_HEADS, tm, w), lambda i: (0, i, 0))
    out_shape = (
        jax.ShapeDtypeStruct((A_HEADS, t, HEAD_DIM), BF16),
        jax.ShapeDtypeStruct((t, A_WIDTH), F32), jax.ShapeDtypeStruct((A_HEADS, t, HEAD_DIM), BF16),
        jax.ShapeDtypeStruct((t, A_WIDTH), F32), jax.ShapeDtypeStruct((A_HEADS, t, LANES), BF16),
        jax.ShapeDtypeStruct((t, IDX_HEADS * IDX_DIM), BF16),
        jax.ShapeDtypeStruct((t, LANES), F32),
        jax.ShapeDtypeStruct((t, IDX_DIM), BF16),
        jax.ShapeDtypeStruct((t, IDX_HEADS * LANES), F32),
        jax.ShapeDtypeStruct((t, SHIFT_W), F32),
        jax.ShapeDtypeStruct((1 if seq_mode else t, SHIFT_W), F32),
        jax.ShapeDtypeStruct((t, 2 * D_MODEL), F32),
    )
    out_specs = (head(HEAD_DIM), row(A_WIDTH), head(HEAD_DIM), row(A_WIDTH), head(LANES),
                 row(IDX_HEADS * IDX_DIM), row(LANES), row(IDX_DIM), row(IDX_HEADS * LANES), row(SHIFT_W),
                 pl.BlockSpec((1, SHIFT_W), lambda i: (0, 0)) if seq_mode else row(SHIFT_W), row(2 * D_MODEL))
    return pl.pallas_call(
        functools.partial(_inproj_kernel, seq_mode),
        grid=(t // tm,),
        in_specs=[row(D_MODEL), const(norm_g), const(w_packed), row(LANES), row(LANES), row(LANES),
                  const(mu), prev_spec],
        out_specs=out_specs, out_shape=out_shape,
        scratch_shapes=[pltpu.VMEM((1, SHIFT_W), F32)],
        compiler_params=_cparams(("arbitrary",)),
        name="inproj_seq" if seq_mode else "inproj_batch",
    )(x, norm_g, w_packed, rope_c, rope_sa, rope_sb, mu, prev)


def _rope_tables(pos):
    half = ROT_DIM // 2
    inv = ROPE_THETA ** (-jnp.arange(half, dtype=F32) / half)
    d = np.arange(LANES) % HEAD_DIM
    inv_lane = jnp.where(d < ROT_DIM, inv[d % half], 0.0)
    ang = pos.astype(F32)[:, None] * inv_lane[None, :]
    sin = jnp.sin(ang)
    return (jnp.cos(ang), jnp.where((d >= half) & (d < ROT_DIM), sin, 0.0), jnp.where(d < half, -sin, 0.0))


def _pack_w_in(w_in):
    lead = QI_END + IDX_DIM + IDX_HEADS
    pad = jnp.zeros((D_MODEL, KIWI_END - lead), w_in.dtype)
    return jnp.concatenate([w_in[:, :lead], pad, w_in[:, lead:]], 1).astype(BF16)


DSA_TQ = 256
DSA_TK = 512
THR_ROWS = 128
THR_WARM_STEPS = 22
THR_SLOT_TARGET = 12
THR_CHEAP_STEPS = 1
THR_UNROLL = DSA_TK // LANES
THR_SLOTS = 4
THR_HALF = 64


def _causal_pairs(t, tq, tk):
    qi, kj, last = [], [], []
    for i in range(t // tq):
        nk = (i * tq + tq - 1) // tk + 1
        for j in range(nk):
            qi.append(i), kj.append(j), last.append(int(j == nk - 1))
    return (jnp.asarray(qi, jnp.int32), jnp.asarray(kj, jnp.int32), jnp.asarray(last, jnp.int32))


def _lane_tile(x, width):
    return jnp.concatenate([x] * (width // LANES), axis=1)


def _index_scores(qi, ki, wi, q0, k0):
    tq, tk = qi.shape[0], ki.shape[0]
    s = None
    for h in range(IDX_HEADS):
        sh = jnp.maximum(_dot_t(qi[:, h * IDX_DIM:(h + 1) * IDX_DIM], ki), 0.0)
        sh = sh * _lane_tile(wi[:, h * LANES:(h + 1) * LANES], tk)
        s = sh if s is None else s + sh
    qpos = q0 + lax.broadcasted_iota(jnp.int32, (tq, tk), 0)
    kpos = k0 + lax.broadcasted_iota(jnp.int32, (tq, tk), 1)
    return jnp.where(kpos <= qpos, s, -jnp.inf)


def _stat_update(stats, tile):
    mn, mx, nz = stats
    return (jnp.minimum(mn, jnp.where(tile == -jnp.inf, jnp.inf, tile)), jnp.maximum(mx, tile),
            nz + jnp.where(tile == 0.0, 1.0, 0.0))


def _stat_init(shape):
    return jnp.full(shape, jnp.inf, F32), jnp.full(shape, -jnp.inf, F32), jnp.zeros(shape, F32)


def _stat_rows(stats):
    mn, mx, nz = stats
    return (jnp.min(mn, axis=-1, keepdims=True), jnp.max(mx, axis=-1, keepdims=True),
            jnp.sum(nz, axis=-1, keepdims=True))


def _kth_largest_rows(sc_ref, r0, ncol, kf, n_finite, row_stats):
    n_rows = kf.shape[0]
    half = min(THR_HALF, n_rows)
    rows = pl.ds(r0, n_rows)
    shape = (n_rows, LANES)

    def col_reduce(fn, init, combine):
        def body(c, acc):
            for u in range(THR_UNROLL):
                acc = combine(acc, fn(sc_ref[c * THR_UNROLL + u, rows, :], 1.0))
            return acc
        return lax.fori_loop(0, ncol // THR_UNROLL, body, jnp.full(shape, init, F32))

    def count_where(reduce, pred):
        return jnp.sum(reduce(lambda b, w: jnp.where(pred(b), w, 0.0), 0.0, jnp.add), axis=-1, keepdims=True)

    def count_gt(reduce, m):
        mb = jnp.broadcast_to(m, shape)
        return count_where(reduce, lambda b: b > mb)

    def snap(reduce, lx, hi):
        lb, hb = jnp.broadcast_to(lx, shape), jnp.broadcast_to(hi, shape)
        lo_acc = reduce(lambda b, w: jnp.where(b > lb, b, jnp.inf), jnp.inf, jnp.minimum)
        hi_acc = reduce(lambda b, w: jnp.where(b <= hb, b, -jnp.inf), -jnp.inf, jnp.maximum)
        return jnp.min(lo_acc, axis=-1, keepdims=True), jnp.max(hi_acc, axis=-1, keepdims=True)

    def bisect(reduce, k, bounds):
        lx, hi, a, b, c_lx, c_hi = bounds
        m = 0.5 * a + 0.5 * b
        c = count_gt(reduce, m)
        up = c >= k
        return (jnp.where(up, m, lx), jnp.where(up, hi, m), jnp.where(up, m, a), jnp.where(up, b, m),
                jnp.where(up, c, c_lx), jnp.where(up, c_hi, c))

    def search(reduce, k, lx, hi):
        def round_(state):
            lx, hi, _, _ = state
            d_lo, d_hi = snap(reduce, lx, hi)
            m = 0.5 * d_lo + 0.5 * d_hi
            m = jnp.where(m < d_hi, m, d_lo)
            up = count_gt(reduce, m) >= k
            lx, hi = jnp.where(up, m, lx), jnp.where(up, hi, m)
            bounds = (lx, hi, jnp.maximum(lx, d_lo), jnp.minimum(hi, d_hi), jnp.zeros_like(lx), jnp.zeros_like(lx))
            for _ in range(THR_CHEAP_STEPS):
                bounds = bisect(reduce, k, bounds)
            open_rows = jnp.max(jnp.where(d_lo < d_hi, 1.0, 0.0))
            return bounds[0], bounds[1], d_lo, open_rows
        init = (lx, hi, jnp.zeros((n_rows, 1), F32), jnp.float32(1.0))
        return lax.while_loop(lambda s: s[3] > 0.0, round_, init)[2]

    lx = jnp.full((n_rows, 1), -jnp.inf, F32)
    hi = jnp.full((n_rows, 1), jnp.inf, F32)
    d_lo, d_hi, n_zero = row_stats
    zeros_inside = lambda lo, up: jnp.where(jnp.logical_and(lo < 0.0, up >= 0.0), n_zero, 0.0)

    def warm(state):
        bounds = bisect(col_reduce, kf, state[:6])
        held = bounds[4] - bounds[5] - zeros_inside(bounds[0], bounds[1])
        crowded = jnp.max(held) > float(THR_SLOT_TARGET)
        return (*bounds, state[6] + 1, jnp.logical_and(crowded, state[6] + 1 < THR_WARM_STEPS))
    warm_init = (lx, hi, d_lo, d_hi, n_finite, jnp.zeros((n_rows, 1), F32), jnp.int32(0), jnp.bool_(True))
    lx, hi, _, _, _, c_hi, _, _ = lax.while_loop(lambda st: st[7], warm, warm_init)

    def compact(h0):
        hrows = pl.ds(r0 + h0, half)
        hshape = (half, LANES)
        lb = jnp.broadcast_to(lx[h0:h0 + half], hshape)
        hb = jnp.broadcast_to(hi[h0:h0 + half], hshape)

        def body(c, carry):
            cnt, bufs = carry[0], list(carry[1:])
            for u in range(THR_UNROLL):
                v = sc_ref[c * THR_UNROLL + u, hrows, :]
                inside = jnp.where(v == 0.0, jnp.inf, jnp.where(v > lb, v, jnp.inf)) <= hb
                key = jnp.where(inside, cnt, -1.0)
                bufs = [jnp.where(key == float(slot), v, buf) for slot, buf in enumerate(bufs)]
                cnt = cnt + jnp.where(inside, 1.0, 0.0)
            return (cnt, *bufs)
        empty = jnp.full(hshape, -jnp.inf, F32)
        return lax.fori_loop(0, ncol // THR_UNROLL, body, (jnp.zeros(hshape, F32),) + (empty,) * THR_SLOTS)

    halves = [compact(h0) for h0 in range(0, n_rows, half)]
    cnt = jnp.concatenate([h[0] for h in halves], axis=0)
    bufs = [jnp.concatenate([h[1 + slot] for h in halves], axis=0) for slot in range(THR_SLOTS)]

    z_in = zeros_inside(lx, hi)
    lane0 = lax.broadcasted_iota(jnp.int32, shape, 1) == 0
    zero_tile = jnp.where(jnp.logical_and(lane0, z_in > 0.0), 0.0, -jnp.inf)
    zero_mult = jnp.where(lane0, z_in, 0.0)

    def small_reduce(fn, init, combine):
        acc = combine(jnp.full(shape, init, F32), fn(zero_tile, zero_mult))
        for buf in bufs:
            acc = combine(acc, fn(buf, 1.0))
        return acc

    def from_slots():
        thr = search(small_reduce, kf - c_hi, lx, hi)
        tb = jnp.broadcast_to(thr, shape)
        return (thr, c_hi + count_where(small_reduce, lambda b: b > tb),
                c_hi + count_where(small_reduce, lambda b: b >= tb))

    def from_all():
        thr = search(col_reduce, kf, lx, hi)
        tb = jnp.broadcast_to(thr, shape)
        return thr, count_where(col_reduce, lambda b: b > tb), count_where(col_reduce, lambda b: b >= tb)

    return lax.cond(jnp.max(cnt) > float(THR_SLOTS), from_all, from_slots)


def _tie_cutoff_rows(sc_ref, r0, ncol, thr, need):
    n_rows = thr.shape[0]
    rows = pl.ds(r0, n_rows)
    shape = (n_rows, LANES)
    tb, nb = jnp.broadcast_to(thr, shape), jnp.broadcast_to(need, shape)
    lane = lax.broadcasted_iota(jnp.int32, shape, 1).astype(F32)
    ri = lax.broadcasted_iota(jnp.int32, (LANES, LANES), 0)
    ci = lax.broadcasted_iota(jnp.int32, (LANES, LANES), 1)
    prefix = jnp.where(ri <= ci, 1.0, 0.0).astype(BF16)
    total = jnp.ones((LANES, LANES), BF16)

    def body(c, carry):
        run, best = carry
        eq = sc_ref[c, rows, :] == tb
        hits = jnp.where(eq, 1.0, 0.0).astype(BF16)
        found = jnp.logical_and(eq, run + _dot(hits, prefix) == nb)
        best = jnp.where(found, lane + (c * LANES).astype(F32), best)
        return run + _dot(hits, total), best
    _, best = lax.fori_loop(0, ncol, body, (jnp.zeros(shape, F32), jnp.full(shape, -1.0, F32)))
    return jnp.max(best, axis=-1, keepdims=True).astype(jnp.int32)


def _dsa_thr_kernel(qi_blk, kj_blk, last_blk, qi_ref, wi_ref, ki_ref, thr_ref, cut_ref, sc_ref, stat_ref):
    p = pl.program_id(0)
    i, j = qi_blk[p], kj_blk[p]
    tq, tk = qi_ref.shape[0], ki_ref.shape[0]
    s = _index_scores(qi_ref[...], ki_ref[...], wi_ref[...], i * tq, j * tk)

    @pl.when(j == 0)
    def _():
        for slot, init in enumerate(_stat_init((tq, LANES))):
            stat_ref[slot] = init

    stats = (stat_ref[0], stat_ref[1], stat_ref[2])
    for u in range(tk // LANES):
        tile = s[:, u * LANES:(u + 1) * LANES]
        sc_ref[j * (tk // LANES) + u] = tile
        stats = _stat_update(stats, tile)
    for slot, value in enumerate(stats):
        stat_ref[slot] = value
    n_keys = sc_ref.shape[0] * LANES

    @pl.when(last_blk[p] == 1)
    def _():
        ncol = (j + 1) * (tk // LANES)
        for g in range(tq // THR_ROWS):
            r0 = g * THR_ROWS
            qpos = i * tq + r0 + lax.broadcasted_iota(jnp.int32, (THR_ROWS, 1), 0)
            kf = jnp.minimum(qpos + 1, TOPK_MAX).astype(F32)
            row_stats = _stat_rows(tuple(stat_ref[slot, r0:r0 + THR_ROWS, :] for slot in range(3)))
            thr, n_gt, n_ge = _kth_largest_rows(sc_ref, r0, ncol, kf, (qpos + 1).astype(F32), row_stats)
            thr_ref[r0:r0 + THR_ROWS, :] = jnp.broadcast_to(thr, (THR_ROWS, LANES))
            cut_ref[r0:r0 + THR_ROWS, :] = jnp.full((THR_ROWS, LANES), n_keys, jnp.int32)

            @pl.when(jnp.max(n_ge - kf) > 0.0)
            def _():
                cut = _tie_cutoff_rows(sc_ref, r0, ncol, thr, kf - n_gt)
                cut = jnp.where(n_ge > kf, cut, n_keys)
                cut_ref[r0:r0 + THR_ROWS, :] = jnp.broadcast_to(cut, (THR_ROWS, LANES))


ATTN_ROWS = 64


def _dsa_attn_kernel(qi_blk, kj_blk, last_blk, q_ref, qi_ref, wi_ref, thr_ref, cut_ref, ki_ref, k_ref, vx_ref,
                     o_ref, m_ref, acc_ref, bias_ref, s_ref, p_ref, alpha_ref):
    p = pl.program_id(0)
    i, j = qi_blk[p], kj_blk[p]
    tq, tk = q_ref.shape[1], k_ref.shape[1]

    @pl.when(j == 0)
    def _():
        m_ref[...] = jnp.full(m_ref.shape, NEG_BIG, F32)
        acc_ref[...] = jnp.zeros(acc_ref.shape, F32)

    s = _index_scores(qi_ref[...], ki_ref[...], wi_ref[...], i * tq, j * tk)
    thr = _lane_tile(thr_ref[...], tk)
    kpos = j * tk + lax.broadcasted_iota(jnp.int32, (tq, tk), 1)
    sel = jnp.logical_or(s > thr, jnp.logical_and(s == thr, kpos <= _lane_tile(cut_ref[...], tk)))
    bias_ref[...] = jnp.where(sel, 0.0, NEG_BIG)

    def logits(h):
        s_ref[h % 2] = _dot_t(q_ref[h], k_ref[h])

    def softmax(h):
        slot = h % 2
        for r0 in range(0, tq, ATTN_ROWS):
            rows = slice(r0, r0 + ATTN_ROWS)
            x = s_ref[slot, rows, :] + bias_ref[rows, :]
            m_prev = m_ref[h, rows, :]
            m_new = jnp.maximum(m_prev, jnp.max(x, axis=-1, keepdims=True))
            p_ref[slot, rows, :] = jnp.exp2(x - _lane_tile(m_new, tk)).astype(BF16)
            alpha_ref[slot, rows, :] = jnp.exp2(m_prev - m_new)
            m_ref[h, rows, :] = m_new

    def values(h):
        acc_ref[h] = alpha_ref[h % 2] * acc_ref[h] + _dot(p_ref[h % 2], vx_ref[h])

    logits(0)
    logits(1)
    softmax(0)

    def head_step(h, carry):
        values(h - 2)
        softmax(h - 1)
        logits(h)
        return carry
    lax.fori_loop(2, A_HEADS, head_step, 0)
    softmax(A_HEADS - 1)
    values(A_HEADS - 2)
    values(A_HEADS - 1)

    @pl.when(last_blk[p] == 1)
    def _():
        o_ref[...] = jnp.concatenate(
            [acc_ref[h][:, 0:HEAD_DIM] / acc_ref[h][:, HEAD_DIM:HEAD_DIM + 1] for h in range(A_HEADS)], axis=1)


def _dsa_prompt(q3, k3, vx3, qi, kiwi, ki):
    t = qi.shape[0]
    tq, tk = min(DSA_TQ, t), min(DSA_TK, t)
    assert t % tq == 0 and t % tk == 0 and tq % THR_ROWS == 0
    pairs = _causal_pairs(t, tq, tk)
    n_pairs = pairs[0].shape[0]
    qrow = lambda w: pl.BlockSpec((tq, w), lambda p, qb, kb, lb: (qb[p], 0))
    krow = lambda w: pl.BlockSpec((tk, w), lambda p, qb, kb, lb: (kb[p], 0))
    thr, cut = pl.pallas_call(
        _dsa_thr_kernel,
        grid_spec=pltpu.PrefetchScalarGridSpec(
            num_scalar_prefetch=3, grid=(n_pairs,),
            in_specs=[qrow(IDX_HEADS * IDX_DIM), qrow(IDX_HEADS * LANES), krow(IDX_DIM)],
            out_specs=(qrow(LANES), qrow(LANES)),
            scratch_shapes=[pltpu.VMEM((t // LANES, tq, LANES), F32), pltpu.VMEM((3, tq, LANES), F32)]),
        out_shape=(jax.ShapeDtypeStruct((t, LANES), F32), jax.ShapeDtypeStruct((t, LANES), jnp.int32)),
        compiler_params=_cparams(("arbitrary",)), name="dsa_prompt_threshold",
    )(*pairs, qi, kiwi, ki)
    qhead = pl.BlockSpec((A_HEADS, tq, HEAD_DIM), lambda p, qb, kb, lb: (0, qb[p], 0))
    khead = lambda w: pl.BlockSpec((A_HEADS, tk, w), lambda p, qb, kb, lb: (0, kb[p], 0))
    return pl.pallas_call(
        _dsa_attn_kernel,
        grid_spec=pltpu.PrefetchScalarGridSpec(
            num_scalar_prefetch=3, grid=(n_pairs,),
            in_specs=[qhead, qrow(IDX_HEADS * IDX_DIM), qrow(IDX_HEADS * LANES), qrow(LANES), qrow(LANES),
                      krow(IDX_DIM), khead(HEAD_DIM), khead(LANES)],
            out_specs=qrow(A_WIDTH),
            scratch_shapes=[pltpu.VMEM((A_HEADS, tq, LANES), F32), pltpu.VMEM((A_HEADS, tq, LANES), F32),
                            pltpu.VMEM((tq, tk), F32), pltpu.VMEM((2, tq, tk), F32),
                            pltpu.VMEM((2, tq, tk), BF16), pltpu.VMEM((2, tq, LANES), F32)]),
        out_shape=jax.ShapeDtypeStruct((t, A_WIDTH), F32),
        compiler_params=_cparams(("arbitrary",)), name="dsa_prompt_attention",
    )(*pairs, q3, qi, kiwi, thr, cut, ki, k3, vx3)


IDX_ROWS = 16


def _dsa_sample_scores_kernel(pt_ref, qi_ref, wi_ref, kin_ref, cache_ref, o_ref, buf_ref, sem_ref):
    b, nb = pl.program_id(0), pl.num_programs(0)
    n_pages = pt_ref.shape[1]

    def issue(bb, slot):
        for p in range(n_pages):
            pltpu.make_async_copy(cache_ref.at[0, pt_ref[bb, p]], buf_ref.at[slot, p], sem_ref.at[slot]).start()

    @pl.when(b == 0)
    def _():
        issue(0, 0)

    @pl.when(b + 1 < nb)
    def _():
        issue(b + 1, (b + 1) % 2)

    slot = b % 2
    pltpu.make_async_copy(buf_ref.at[slot], buf_ref.at[slot], sem_ref.at[slot]).wait()
    qi, wi = qi_ref[0], wi_ref[0]

    def pages(g, carry):
        rows = []
        for u in range(SUBLANES):
            s = jnp.maximum(_dot(qi, buf_ref[slot, g * SUBLANES + u].astype(BF16)), 0.0) * wi
            rows.append(jnp.sum(s, axis=0, keepdims=True))
        o_ref[0, pl.ds(pl.multiple_of(g * SUBLANES, SUBLANES), SUBLANES), :] = jnp.concatenate(rows, axis=0)
        return carry
    lax.fori_loop(0, n_pages // SUBLANES, pages, 0)
    kin = kin_ref[0].astype(F32)
    s_new = jnp.maximum(jnp.sum(qi.astype(F32) * kin, axis=-1, keepdims=True), 0.0) * wi
    s_new = jnp.sum(s_new, axis=0, keepdims=True)
    lane = lax.broadcasted_iota(jnp.int32, (SUBLANES, LANES), 1)
    row = lax.broadcasted_iota(jnp.int32, (SUBLANES, LANES), 0)
    o_ref[0, n_pages:n_pages + SUBLANES, :] = jnp.where(jnp.logical_and(lane == 0, row == 0), s_new, -jnp.inf)


def _dsa_sample_scores(page_table, qi3, wi3, ki_new, cache_idx_t):
    bsz, n_pages = page_table.shape
    assert n_pages % SUBLANES == 0
    n_rows = n_pages + SUBLANES
    blk = lambda a: pl.BlockSpec((1,) + a.shape[1:], lambda b, pt: (b, 0, 0))
    return pl.pallas_call(
        _dsa_sample_scores_kernel,
        grid_spec=pltpu.PrefetchScalarGridSpec(
            num_scalar_prefetch=1, grid=(bsz,),
            in_specs=[blk(qi3), blk(wi3), blk(ki_new), pl.BlockSpec(memory_space=pl.ANY)],
            out_specs=pl.BlockSpec((1, n_rows, LANES), lambda b, pt: (b, 0, 0)),
            scratch_shapes=[pltpu.VMEM((2, n_pages, IDX_DIM, PAGE_SIZE), F32), pltpu.SemaphoreType.DMA((2,))]),
        out_shape=jax.ShapeDtypeStruct((bsz, n_rows, LANES), F32),
        compiler_params=_cparams(("arbitrary",)), name="dsa_sample_scores",
    )(page_table, qi3, wi3, ki_new, cache_idx_t)


def _topk_mask_kernel(k_sel, n_finite, sc_ref, mask_ref):
    n_tiles, bsz, _ = sc_ref.shape
    width = n_tiles * LANES
    ncol = n_tiles + 0 * pl.program_id(0)
    shape = (bsz, LANES)
    kf = jnp.full((bsz, 1), float(k_sel), F32)
    row_stats = _stat_rows(lax.fori_loop(0, ncol, lambda c, st: _stat_update(st, sc_ref[c]), _stat_init(shape)))
    thr, n_gt, n_ge = _kth_largest_rows(sc_ref, 0, ncol, kf, jnp.full((bsz, 1), float(n_finite), F32), row_stats)
    cut = lax.cond(jnp.max(n_ge - kf) > 0.0,
                   lambda: jnp.where(n_ge > kf, _tie_cutoff_rows(sc_ref, 0, ncol, thr, kf - n_gt), width),
                   lambda: jnp.full((bsz, 1), width, jnp.int32))
    tb, cb = jnp.broadcast_to(thr, shape), jnp.broadcast_to(cut, shape)
    lane = lax.broadcasted_iota(jnp.int32, shape, 1)

    def mark(c, carry):
        v = sc_ref[c]
        sel = jnp.logical_or(v > tb, jnp.logical_and(v == tb, lane + c * LANES <= cb))
        mask_ref[c] = jnp.where(sel, 1.0, 0.0)
        return carry
    lax.fori_loop(0, ncol, mark, 0)


def _topk_mask(scores_t, k_sel, n_finite):
    assert (scores_t.shape[0] * LANES) % DSA_TK == 0 and n_finite >= k_sel
    whole = pl.BlockSpec(scores_t.shape, lambda i: (0, 0, 0))
    return pl.pallas_call(
        functools.partial(_topk_mask_kernel, k_sel, n_finite), grid=(1,),
        in_specs=[whole], out_specs=whole,
        out_shape=jax.ShapeDtypeStruct(scores_t.shape, F32),
        compiler_params=_cparams(("arbitrary",)), name="dsa_sample_topk",
    )(scores_t)


SAMPLE_PAGES_PER_STEP = 16


def _dsa_sample_attn_kernel(pt_ref, qt_ref, knt_ref, vnt_ref, mask_ref, ck_ref, cv_ref, o_ref,
                            kbuf, vbuf, sem_ref, m_ref, l_ref, acc_ref):
    b, c = pl.program_id(0), pl.program_id(1)
    nb, nc = pl.num_programs(0), pl.num_programs(1)
    pps = SAMPLE_PAGES_PER_STEP
    n_pages = pt_ref.shape[1]
    step = b * nc + c

    def issue(s, slot):
        bb, cc = s // nc, s % nc
        for i in range(pps):
            page = pt_ref[bb, cc * pps + i]
            pltpu.make_async_copy(ck_ref.at[0, page], kbuf.at[slot, i], sem_ref.at[0, slot]).start()
            pltpu.make_async_copy(cv_ref.at[0, page], vbuf.at[slot, i], sem_ref.at[1, slot]).start()

    @pl.when(step == 0)
    def _():
        issue(0, 0)

    @pl.when(step + 1 < nb * nc)
    def _():
        issue(step + 1, (step + 1) % 2)

    slot = step % 2
    pltpu.make_async_copy(kbuf.at[slot], kbuf.at[slot], sem_ref.at[0, slot]).wait()
    pltpu.make_async_copy(vbuf.at[slot], vbuf.at[slot], sem_ref.at[1, slot]).wait()

    @pl.when(c == 0)
    def _():
        m_ref[...] = jnp.full(m_ref.shape, NEG_BIG, F32)
        l_ref[...] = jnp.zeros(l_ref.shape, F32)
        acc_ref[...] = jnp.zeros(acc_ref.shape, F32)

    keep = mask_ref[0, pl.ds(pl.multiple_of(c * pps, pps), pps), :] > 0.0
    qt = qt_ref[0].astype(F32)
    heads = range(A_HEADS)
    qcols = [jnp.broadcast_to(qt[:, h:h + 1], (HEAD_DIM, PAGE_SIZE)) for h in heads]
    lg = [jnp.concatenate([jnp.sum(kbuf[slot, i, h] * qcols[h], axis=0, keepdims=True) for i in range(pps)], axis=0)
          for h in heads]
    lg = [jnp.where(keep, x, NEG_BIG) for x in lg]
    m_old = [m_ref[h] for h in heads]
    m_new = [jnp.maximum(m_old[h], jnp.max(lg[h], axis=(0, 1), keepdims=True)) for h in heads]
    alpha = [jnp.exp2(m_old[h] - m_new[h]) for h in heads]
    p = [jnp.where(keep, jnp.exp2(lg[h] - m_new[h][0:1, :]), 0.0) for h in heads]
    for h in heads:
        l_ref[h] = alpha[h] * l_ref[h] + jnp.sum(p[h], axis=(0, 1), keepdims=True)
        m_ref[h] = m_new[h]
    for h in heads:
        pv = vbuf[slot, 0, h] * p[h][0:1, :]
        for i in range(1, pps):
            pv = pv + vbuf[slot, i, h] * p[h][i:i + 1, :]
        acc_ref[h] = alpha[h][0:1, :] * acc_ref[h] + pv

    @pl.when(c == nc - 1)
    def _():
        lane = lax.broadcasted_iota(jnp.int32, (HEAD_DIM, LANES), 1)
        keep_new = mask_ref[0, n_pages:n_pages + 1, 0:1] > 0.0
        out = jnp.zeros((HEAD_DIM, LANES), F32)
        for h in range(A_HEADS):
            lg_new = jnp.sum(knt_ref[0][:, h:h + 1] * qt[:, h:h + 1], axis=0, keepdims=True)
            lg_new = jnp.where(keep_new, lg_new, NEG_BIG)
            m_old = m_ref[h][0:1, 0:1]
            m_new = jnp.maximum(m_old, lg_new)
            alpha = jnp.exp2(m_old - m_new)
            p_new = jnp.where(keep_new, jnp.exp2(lg_new - m_new), 0.0)
            l_fin = alpha * l_ref[h][0:1, 0:1] + p_new
            col = alpha * jnp.sum(acc_ref[h], axis=-1, keepdims=True) + p_new * vnt_ref[0][:, h:h + 1]
            out = jnp.where(lane == h, col / l_fin, out)
        o_ref[0] = out


def _dsa_sample_attn(page_table, q_t, k_new_t, v_new_t, mask3, cache_k_t, cache_v_t):
    bsz, n_pages = page_table.shape
    pps = SAMPLE_PAGES_PER_STEP
    assert n_pages % pps == 0
    col = lambda: pl.BlockSpec((1, HEAD_DIM, A_HEADS), lambda b, c, pt: (b, 0, 0))
    page_buf = pltpu.VMEM((2, pps, A_HEADS, HEAD_DIM, PAGE_SIZE), F32)
    return pl.pallas_call(
        _dsa_sample_attn_kernel,
        grid_spec=pltpu.PrefetchScalarGridSpec(
            num_scalar_prefetch=1, grid=(bsz, n_pages // pps),
            in_specs=[col(), col(), col(), pl.BlockSpec((1,) + mask3.shape[1:], lambda b, c, pt: (b, 0, 0)),
                      pl.BlockSpec(memory_space=pl.ANY), pl.BlockSpec(memory_space=pl.ANY)],
            out_specs=pl.BlockSpec((1, HEAD_DIM, LANES), lambda b, c, pt: (b, 0, 0)),
            scratch_shapes=[page_buf, page_buf, pltpu.SemaphoreType.DMA((2, 2)),
                            pltpu.VMEM((A_HEADS, SUBLANES, LANES), F32), pltpu.VMEM((A_HEADS, SUBLANES, LANES), F32),
                            pltpu.VMEM((A_HEADS, HEAD_DIM, LANES), F32)]),
        out_shape=jax.ShapeDtypeStruct((bsz, HEAD_DIM, LANES), F32),
        compiler_params=_cparams(("arbitrary", "arbitrary")), name="dsa_sample_attention",
    )(page_table, q_t, k_new_t, v_new_t, mask3, cache_k_t, cache_v_t)


def _head_ones():
    h = np.arange(B_WIDTH) // B_HEAD_DIM
    return jnp.asarray((h[:, None] == h[None, :]).astype(np.float32))


def _rwkv_pre_kernel(sh_ref, w0_ref, wup_ref, a0_ref, aup_ref, gup_ref, kk_ref_, ka_ref, g1_ref,
                     ld_ref, kk_ref, kb_ref, kp_ref, g_ref):
    k = sh_ref[:, B_WIDTH:2 * B_WIDTH]
    o = 3 * B_WIDTH
    xw = sh_ref[:, o:o + DECAY_LORA]
    xa = sh_ref[:, o + DECAY_LORA:o + DECAY_LORA + AAA_LORA]
    xg = sh_ref[:, o + DECAY_LORA + AAA_LORA:SHIFT_W]
    w = w0_ref[...] + _dot3(jnp.tanh(xw), wup_ref[...])
    logw = -jax.nn.softplus(-w) - 0.5
    ld_ref[...] = -jnp.exp(logw)
    a = jax.nn.sigmoid(a0_ref[...] + _dot3(xa, aup_ref[...]))
    g_ref[...] = _dot3(jax.nn.sigmoid(xg), gup_ref[...])
    kk = k * kk_ref_[...]
    kk = kk / jnp.maximum(jnp.sqrt(_dot3(kk * kk, g1_ref[...])), 1e-12)
    kk_ref[...] = kk
    kb_ref[...] = kk * a
    kp_ref[...] = k * (1.0 + (a - 1.0) * ka_ref[...])


def _rwkv_pre(shifted, w0, w_lora_up, a0, a_lora_up, g_lora_up, k_k, k_a, tm):
    t = shifted.shape[0]
    row = lambda w: pl.BlockSpec((tm, w), lambda i: (i, 0))
    const = lambda a: pl.BlockSpec(a.shape, lambda i: (0, 0))
    g1 = _head_ones()
    args = (shifted, w0, w_lora_up, a0, a_lora_up, g_lora_up, k_k, k_a, g1)
    out = jax.ShapeDtypeStruct((t, B_WIDTH), F32)
    return pl.pallas_call(
        _rwkv_pre_kernel, grid=(t // tm,),
        in_specs=[row(SHIFT_W)] + [const(a) for a in args[1:]],
        out_specs=(row(B_WIDTH),) * 5, out_shape=(out,) * 5,
        compiler_params=_cparams(("arbitrary",)), name="rwkv_prep",
    )(*args)


def _rwkv_seq_kernel(r_ref, v_ref, ld_ref, kk_ref, kb_ref, kp_ref, y_ref, so_ref, st_ref):
    c = CHUNK

    @pl.when(pl.program_id(0) == 0)
    def _():
        st_ref[...] = jnp.zeros(st_ref.shape, F32)

    ri = lax.broadcasted_iota(jnp.int32, (c, c), 0)
    ci = lax.broadcasted_iota(jnp.int32, (c, c), 1)
    incl, strict, eye = ri >= ci, ri > ci, ri == ci
    ld = ld_ref[...]
    tri = jnp.where(incl, 1.0, 0.0).astype(BF16)
    ld_hi = ld.astype(BF16)
    rest = ld - ld_hi.astype(F32)
    ld_mid = rest.astype(BF16)
    ld_lo = (rest - ld_mid.astype(F32)).astype(BF16)
    cum = _dot(tri, ld_hi) + (_dot(tri, ld_mid) + _dot(tri, ld_lo))
    cum_c = cum[c - 1:c, :]
    e_neg, e_rem = jnp.exp(-cum), jnp.exp(cum_c - cum)
    kk, kb, kp = kk_ref[...], kb_ref[...], kp_ref[...]
    a_all = -kk * jnp.exp(cum - ld)
    r_all = r_ref[...] * jnp.exp(cum)
    bt_all, kt_all = kb * e_neg, kp * e_neg
    bh_all, kh_all = kb * e_rem, kp * e_rem
    pc = jnp.exp(cum_c)
    v_all = v_ref[...]
    n = B_HEAD_DIM
    heads = range(B_HEADS)
    hsl = [slice(h * n, (h + 1) * n) for h in heads]
    p = [_dot3_t(jnp.concatenate([a_all[:, s], r_all[:, s]], 0), jnp.concatenate([bt_all[:, s], kt_all[:, s]], 0))
         for s in hsl]
    l_ak = [jnp.where(strict, q[:c, c:], 0.0) for q in p]
    m_rb = [jnp.where(incl, q[c:, :c], 0.0) for q in p]
    m_rk = [jnp.where(incl, q[c:, c:], 0.0) for q in p]
    lp = [jnp.where(strict, q[:c, :c], 0.0) for q in p]
    x = [jnp.concatenate([a_all[:, s], _dot3(l_ak[h], v_all[:, s])], 1) for h, s in enumerate(hsl)]
    n_dbl = int(np.log2(c))
    for it in range(n_dbl):
        x = [x[h] + _dot3(lp[h], x[h]) for h in heads]
        if it + 1 < n_dbl:
            lp = [_dot3(lp[h], lp[h]) for h in heads]
    bm = [_dot3(jnp.concatenate([bh_all[:, s].T, m_rb[h]], 0), x[h]) for h, s in enumerate(hsl)]
    kv = [_dot3(jnp.concatenate([kh_all[:, s].T, m_rk[h]], 0), v_all[:, s]) for h, s in enumerate(hsl)]
    for h, s in enumerate(hsl):
        m_c = jnp.where(eye, jnp.broadcast_to(pc[:, s], (n, n)), 0.0) + bm[h][:n, :n]
        g_c = r_all[:, s] + bm[h][n:, :n]
        gs = _dot3(jnp.concatenate([g_c, m_c], 0), st_ref[h])
        y_ref[:, s] = gs[:c, :] + (bm[h][n:, n:] + kv[h][n:, :])
        st_ref[h] = gs[c:, :] + (bm[h][:n, n:] + kv[h][:n, :])
    so_ref[...] = st_ref[...]


def _rwkv_seq(shifted, ld, kk, kb, kp):
    t = shifted.shape[0]
    assert t % CHUNK == 0
    col = lambda j: pl.BlockSpec((CHUNK, B_WIDTH), lambda i: (i, j))
    state = jax.ShapeDtypeStruct((B_HEADS, B_HEAD_DIM, B_HEAD_DIM), F32)
    return pl.pallas_call(
        _rwkv_seq_kernel, grid=(t // CHUNK,),
        in_specs=[col(0), col(2), col(0), col(0), col(0), col(0)],
        out_specs=(col(0), pl.BlockSpec(state.shape, lambda i: (0, 0, 0))),
        out_shape=(jax.ShapeDtypeStruct((t, B_WIDTH), F32), state),
        scratch_shapes=[pltpu.VMEM(state.shape, F32)],
        compiler_params=_cparams(("arbitrary",)), name="rwkv_chunked_scan",
    )(shifted, shifted, ld, kk, kb, kp)


RWKV_STEP_ROWS = 8


def _rwkv_step_kernel(s_ref, r_ref, ld_ref, kk_ref, kb_ref, kp_ref, v_ref, so_ref, y_ref):
    def one(b, carry):
        row = pl.ds(b, 1)
        r, w, kk, kb, kp = r_ref[row, :], jnp.exp(ld_ref[row, :]), kk_ref[row, :], kb_ref[row, :], kp_ref[row, :]
        for h in range(B_HEADS):
            hs = slice(h * B_HEAD_DIM, (h + 1) * B_HEAD_DIM)
            s = s_ref[b, h]
            sa = -jnp.sum(s * kk[:, hs], axis=-1, keepdims=True)
            s2 = s * w[:, hs] + sa * kb[:, hs] + v_ref[b, h] * kp[:, hs]
            so_ref[b, h] = s2
            y_ref[b, h] = jnp.sum(s2 * r[:, hs], axis=-1, keepdims=True)
        return carry
    lax.fori_loop(0, s_ref.shape[0], one, 0)


def _rwkv_step(state, shifted, ld, kk, kb, kp, v_col):
    bsz = state.shape[0]
    nb = RWKV_STEP_ROWS
    assert bsz % nb == 0
    row = lambda: pl.BlockSpec((nb, B_WIDTH), lambda i: (i, 0))
    st = pl.BlockSpec((nb, B_HEADS, B_HEAD_DIM, B_HEAD_DIM), lambda i: (i, 0, 0, 0))
    colv = pl.BlockSpec((nb, B_HEADS, B_HEAD_DIM, 1), lambda i: (i, 0, 0, 0))
    return pl.pallas_call(
        _rwkv_step_kernel, grid=(bsz // nb,),
        in_specs=[st, row(), row(), row(), row(), row(), colv],
        out_specs=(st, colv),
        out_shape=(jax.ShapeDtypeStruct(state.shape, F32),
                   jax.ShapeDtypeStruct((bsz, B_HEADS, B_HEAD_DIM, 1), F32)),
        compiler_params=_cparams(("arbitrary",)), name="rwkv_single_step",
    )(state, shifted, ld, kk, kb, kp, v_col)


def _rwkv_post_kernel(y_ref, r_ref, v_ref, kp_ref, g_ref, rk_ref, gng_ref, gnb_ref, g1_ref, o_ref):
    g1 = g1_ref[...]
    inv_n = 1.0 / B_HEAD_DIM
    y = y_ref[...]
    d = y - _dot3(y, g1) * inv_n
    var = _dot3(d * d, g1) * inv_n
    yn = d * lax.rsqrt(var + GN_EPS) * gng_ref[...] + gnb_ref[...]
    bonus = _dot3(r_ref[...] * kp_ref[...] * rk_ref[...], g1) * v_ref[...]
    o_ref[...] = (yn + bonus) * g_ref[...]


def _rwkv_post(y, shifted, kp, g, r_k, gn_g, gn_b, tm):
    t = y.shape[0]
    col = lambda j: pl.BlockSpec((tm, B_WIDTH), lambda i: (i, j))
    const = lambda a: pl.BlockSpec(a.shape, lambda i: (0, 0))
    g1 = _head_ones()
    return pl.pallas_call(
        _rwkv_post_kernel, grid=(t // tm,),
        in_specs=[col(0), col(0), col(2), col(0), col(0), const(r_k), const(gn_g), const(gn_b), const(g1)],
        out_specs=col(0), out_shape=jax.ShapeDtypeStruct((t, B_WIDTH), F32),
        compiler_params=_cparams(("arbitrary",)), name="rwkv_output_norm",
    )(y, shifted, shifted, kp, g, r_k, gn_g, gn_b, g1)


def _outproj_kernel(x_ref, attn_ref, rw_ref, gate_ref, wa_ref, wb_ref, wo_ref, nf_ref, wr_ref, br_ref,
                    h_ref, hn_ref, comb_ref):
    ga, gb = gate_ref[:, :D_MODEL], gate_ref[:, D_MODEL:]
    merged = ga * _dot(attn_ref[...].astype(BF16), wa_ref[...]) + gb * _dot(rw_ref[...].astype(BF16), wb_ref[...])
    h = x_ref[...] + _dot(merged.astype(BF16), wo_ref[...])
    h_ref[...] = h
    hn = _rms(h, nf_ref[...])
    hn_ref[...] = hn.astype(BF16)
    logits = _dotx(hn, wr_ref[...]) + br_ref[...]
    lane = lax.broadcasted_iota(jnp.int32, logits.shape, 1)
    big = jnp.int32(LANES)
    first = lambda hit: jnp.min(jnp.where(hit, lane, big), axis=-1, keepdims=True)
    is_g = lane < N_GROUPS
    gl = jnp.where(is_g, logits, -jnp.inf)
    gmax = jnp.max(gl, axis=-1, keepdims=True)
    p_group = 1.0 / jnp.sum(jnp.exp(gl - gmax), axis=-1, keepdims=True)
    g_sel = first(gl == gmax)
    e0 = N_GROUPS + g_sel * EXPERTS_PER_GROUP
    el = jnp.where(jnp.logical_and(lane >= e0, lane < e0 + EXPERTS_PER_GROUP), logits, -jnp.inf)
    v1 = jnp.max(el, axis=-1, keepdims=True)
    i1 = first(el == v1)
    el2 = jnp.where(lane == i1, -jnp.inf, el)
    v2 = jnp.max(el2, axis=-1, keepdims=True)
    i2 = first(el2 == v2)
    e21 = jnp.exp(v2 - v1)
    w1 = p_group / (1.0 + e21)
    w2 = p_group * e21 / (1.0 + e21)
    comb_ref[...] = jnp.where(lane == i1 - N_GROUPS, w1, 0.0) + jnp.where(lane == i2 - N_GROUPS, w2, 0.0)


def _outproj(x, attn, rw, gates, wa, wb, wo, norm_ffn, w_router, b_router, tm):
    t = x.shape[0]
    row = lambda w: pl.BlockSpec((tm, w), lambda i: (i, 0))
    const = lambda a: pl.BlockSpec(a.shape, lambda i: (0, 0))
    return pl.pallas_call(
        _outproj_kernel, grid=(t // tm,),
        in_specs=[row(D_MODEL), row(A_WIDTH), row(B_WIDTH), row(2 * D_MODEL), const(wa), const(wb), const(wo),
                  const(norm_ffn), const(w_router), const(b_router)],
        out_specs=(row(D_MODEL), row(D_MODEL), row(LANES)),
        out_shape=(jax.ShapeDtypeStruct((t, D_MODEL), F32), jax.ShapeDtypeStruct((t, D_MODEL), BF16),
                   jax.ShapeDtypeStruct((t, LANES), F32)),
        compiler_params=_cparams(("arbitrary",)), name="outproj_router",
    )(x, attn, rw, gates, wa, wb, wo, norm_ffn, w_router, b_router)


def _moe_kernel(h_ref, hn_ref, comb_ref, wg_ref, wu_ref, wd_ref, nfin_ref, o_ref, acc_ref):
    e = pl.program_id(1)
    hn = hn_ref[...]
    act = jax.nn.silu(_dot(hn, wg_ref[0])) * _dot(hn, wu_ref[0])
    y = _dot(act.astype(BF16), wd_ref[0])
    lane = lax.broadcasted_iota(jnp.int32, comb_ref.shape, 1)
    c_e = jnp.sum(jnp.where(lane == e, comb_ref[...], 0.0), axis=-1, keepdims=True)

    @pl.when(e == 0)
    def _():
        acc_ref[...] = c_e * y

    @pl.when(e > 0)
    def _():
        acc_ref[...] += c_e * y

    @pl.when(e == pl.num_programs(1) - 1)
    def _():
        o_ref[...] = _rms(h_ref[...] + acc_ref[...], nfin_ref[...])


def _moe(h, hn, comb, wg, wu, wd, norm_final, tm):
    t = h.shape[0]
    row = lambda w: pl.BlockSpec((tm, w), lambda i, e: (i, 0))
    return pl.pallas_call(
        _moe_kernel, grid=(t // tm, N_EXPERTS),
        in_specs=[row(D_MODEL), row(D_MODEL), row(LANES),
                  pl.BlockSpec((1, D_MODEL, D_EXPERT), lambda i, e: (e, 0, 0)),
                  pl.BlockSpec((1, D_MODEL, D_EXPERT), lambda i, e: (e, 0, 0)),
                  pl.BlockSpec((1, D_EXPERT, D_MODEL), lambda i, e: (e, 0, 0)),
                  pl.BlockSpec(norm_final.shape, lambda i, e: (0, 0))],
        out_specs=row(D_MODEL), out_shape=jax.ShapeDtypeStruct((t, D_MODEL), F32),
        scratch_shapes=[pltpu.VMEM((tm, D_MODEL), F32)],
        compiler_params=_cparams(("arbitrary", "arbitrary")), name="moe_experts",
    )(h, hn, comb, wg, wu, wd, norm_final)


def _tile(t, pref):
    return min(pref, t)


def _mixer_tail(x, attn, shifted, gates, state_step, weights, ffn):
    (w0, w_lora_up, a0, a_lora_up, g_lora_up, k_k, k_a, r_k, gn_g, gn_b, wa, wb, wo) = weights
    (norm_ffn, w_router, b_router, wg, wu, wd, norm_final) = ffn
    t = x.shape[0]
    tm = _tile(t, 256)
    ld, kk, kb, kp, g = _rwkv_pre(shifted, w0, w_lora_up, a0, a_lora_up, g_lora_up, k_k, k_a, tm)
    if state_step is None:
        y, st = _rwkv_seq(shifted, ld, kk, kb, kp)
        wkv = jnp.swapaxes(st, 1, 2)[None]
    else:
        v_col = shifted[:, 2 * B_WIDTH:3 * B_WIDTH].reshape(t, B_HEADS, B_HEAD_DIM, 1)
        wkv, y_col = _rwkv_step(state_step, shifted, ld, kk, kb, kp, v_col)
        y = y_col.reshape(t, B_WIDTH)
    rw = _rwkv_post(y, shifted, kp, g, r_k, gn_g, gn_b, tm)
    h, hn, comb = _outproj(x, attn, rw, gates, wa, wb, wo, norm_ffn, w_router, b_router, tm)
    out = _moe(h, hn, comb, wg, wu, wd, norm_final, _tile(t, 1024))
    return out, wkv


def kernel(x_prompt, x_sample, cache_k, cache_v, cache_idx_k, state_wkv, state_shift, page_table, norm_mix, w_in, mu_shift, w0, w_lora_up, a0, a_lora_up, g_lora_up, k_k, k_a, r_k, gn_g, gn_b, w_branch_a, w_branch_b, w_out, norm_ffn, w_router_group, b_router_group, w_router_expert, b_router_expert, w_gate, w_up, w_down, norm_final):
    assert w_in.shape[0] == 1, "single-layer kernel"
    bp, tp, _ = x_prompt.shape
    bs, ts, _ = x_sample.shape
    assert bp == 1 and ts == 1
    n_pages = page_table.shape[1]
    past = n_pages * PAGE_SIZE
    row2 = lambda a: a.reshape(1, -1)

    w_packed = _pack_w_in(w_in[0])
    mix_w = (row2(w0[0]), w_lora_up[0], row2(a0[0]), a_lora_up[0], g_lora_up[0], row2(k_k[0]), row2(k_a[0]),
             row2(r_k[0]), row2(gn_g[0]), row2(gn_b[0]),
             w_branch_a[0].astype(BF16), w_branch_b[0].astype(BF16), w_out[0].astype(BF16))
    pad_r = jnp.zeros((D_MODEL, LANES - N_GROUPS - N_EXPERTS), F32)
    w_router = jnp.concatenate([w_router_group[0], w_router_expert[0], pad_r], 1)
    b_router = jnp.concatenate([b_router_group[0], b_router_expert[0], pad_r[0]])[None]
    ffn_w = (row2(norm_ffn[0]), w_router, b_router, w_gate[0].astype(BF16), w_up[0].astype(BF16),
             w_down[0].astype(BF16), row2(norm_final))
    g_mix, mu = row2(norm_mix[0]), row2(mu_shift[0])

    xp = x_prompt[0]
    tabs = _rope_tables(jnp.arange(tp))
    (q3, kf, k3, vf, vx3, qi, kiwi, ki16, wrep, shifted, shift_last, gates) = _inproj(
        xp, g_mix, w_packed, *tabs, mu, jnp.zeros((1, SHIFT_W), F32), True, _tile(tp, 256))
    attn = _dsa_prompt(q3, k3, vx3, qi, wrep, ki16)
    y_p, wkv_p = _mixer_tail(xp, attn, shifted, gates, None, mix_w, ffn_w)

    xs = x_sample[:, 0]
    tabs_s = _rope_tables(jnp.full((bs,), past, jnp.int32))
    (q3_s, kf_s, _, vf_s, _, qi_s, kiwi_s, ki16_s, _, shifted_s, shift_raw_s, gates_s) = _inproj(
        xs, g_mix, w_packed, *tabs_s, mu, state_shift[0], False, _tile(bs, 256))
    pad_h = IDX_ROWS - IDX_HEADS
    qi3 = jnp.pad(qi_s.reshape(bs, IDX_HEADS, IDX_DIM), ((0, 0), (0, pad_h), (0, 0)))
    wi3 = jnp.pad(kiwi_s[:, IDX_DIM:IDX_DIM + IDX_HEADS], ((0, 0), (0, pad_h)))[:, :, None]
    scores = _dsa_sample_scores(page_table, qi3, wi3, ki16_s[:, None, :], jnp.swapaxes(cache_idx_k, 2, 3))
    k_sel = min(TOPK_MAX, (past + ts) // 4)
    mask = jnp.swapaxes(_topk_mask(jnp.swapaxes(scores, 0, 1), k_sel, past + ts), 0, 1)
    cols = lambda a: jnp.swapaxes(a.reshape(bs, A_HEADS, HEAD_DIM), 1, 2)
    page_t = lambda cache: jnp.transpose(cache, (0, 1, 3, 4, 2))
    q_cols = jnp.transpose(q3_s, (1, 2, 0))
    attn_t = _dsa_sample_attn(page_table, q_cols, cols(kf_s), cols(vf_s), mask, page_t(cache_k), page_t(cache_v))
    attn_s = jnp.swapaxes(attn_t[:, :, :A_HEADS], 1, 2).reshape(bs, A_WIDTH)
    y_s, wkv_s = _mixer_tail(xs, attn_s, shifted_s, gates_s, state_wkv[0], mix_w, ffn_w)

    kv5 = lambda a, b, t: a.reshape(1, b, t, A_HEADS, HEAD_DIM)
    return (y_p[None], y_s[:, None],
            kv5(kf, 1, tp), kv5(vf, 1, tp), kiwi[:, :IDX_DIM].reshape(1, 1, tp, IDX_DIM),
            wkv_p[None], shift_last.reshape(1, 1, SHIFT_W),
            kv5(kf_s, bs, 1), kv5(vf_s, bs, 1), kiwi_s[:, :IDX_DIM].reshape(1, bs, 1, IDX_DIM),
            wkv_s[None], shift_raw_s[None])
```
